```python
import jax, jax.numpy as jnp
from jax import lax
import numpy as np


D_MODEL = 4096
BATCH = 2
SEQ = 4096
DEPTH = 2
DEC_BATCH = 32
DEC_SEQ = 32
PAST_LEN = 4096

CHUNK = 64
MIX_WIDTH = D_MODEL
ATT_WIDTH = MIX_WIDTH // 2
CONV_CH = MIX_WIDTH - ATT_WIDTH
HEAD_DIM = 128
N_ATT_HEADS = ATT_WIDTH // HEAD_DIM
CONV_WIDTH = 31
D_FF = -(-8 * D_MODEL // (3 * 256)) * 256
D_IN = 3 * ATT_WIDTH + N_ATT_HEADS + 2 * CONV_CH
Q_BLOCK = 128
EPS = 1e-6

kernel_name = 'fox_conformer_hymba_stream_step'


def rmsnorm(x, g):
    xf = x.astype(jnp.float32)
    y = xf * lax.rsqrt(jnp.mean(xf * xf, axis=-1, keepdims=True) + EPS)
    return (y * g.astype(jnp.float32)).astype(x.dtype)


def layernorm(x, g, b):
    xf = x.astype(jnp.float32)
    mu = jnp.mean(xf, axis=-1, keepdims=True)
    xc = xf - mu
    y = xc * lax.rsqrt(jnp.mean(xc * xc, axis=-1, keepdims=True) + EPS)
    return (y * g.astype(jnp.float32) + b.astype(jnp.float32)).astype(x.dtype)


def mixer_inputs(x, ln_g, w_in, b_f):
    B, T, _ = x.shape
    h = rmsnorm(x, ln_g)
    z = jnp.einsum('btd,de->bte', h, w_in)
    q, k, v, f, glu_in = jnp.split(
        z, [ATT_WIDTH, 2 * ATT_WIDTH, 3 * ATT_WIDTH, 3 * ATT_WIDTH + N_ATT_HEADS], axis=-1)
    q = q.reshape(B, T, N_ATT_HEADS, HEAD_DIM)
    k = k.reshape(B, T, N_ATT_HEADS, HEAD_DIM)
    v = v.reshape(B, T, N_ATT_HEADS, HEAD_DIM)
    logf = jax.nn.log_sigmoid((f + b_f).astype(jnp.float32))
    return q, k, v, logf, glu_in


def fox_attention(q, k, v, cum_q, cum_k, q_offset):
    B, T, H, Dh = q.shape
    S = k.shape[1]
    blk = Q_BLOCK if T % Q_BLOCK == 0 else T
    nb = T // blk
    scale = HEAD_DIM ** -0.5
    kf = k.astype(jnp.float32)
    vf = v.astype(jnp.float32)
    ck = jnp.transpose(cum_k, (0, 2, 1))
    kpos = jnp.arange(S)
    qb = q.reshape(B, nb, blk, H, Dh).swapaxes(0, 1)
    cqb = cum_q.reshape(B, nb, blk, H).swapaxes(0, 1)

    def one_block(args):
        i, q_i, cq_i = args
        s = jnp.einsum('bqhd,bkhd->bhqk', q_i.astype(jnp.float32), kf) * scale
        s = s + jnp.transpose(cq_i, (0, 2, 1))[..., None] - ck[:, :, None, :]
        qpos = q_offset + i * blk + jnp.arange(blk)
        s = jnp.where(kpos[None, :] <= qpos[:, None], s, -jnp.inf)
        p = jax.nn.softmax(s, axis=-1)
        return jnp.einsum('bhqk,bkhd->bqhd', p, vf)

    out = lax.map(one_block, (jnp.arange(nb), qb, cqb))
    return out.swapaxes(0, 1).reshape(B, T, H * Dh).astype(q.dtype)


def conformer_conv(glu_in, prefix, w_dw, b_dw, cln_g, cln_b):
    a, g = jnp.split(glu_in, 2, axis=-1)
    u = a * jax.nn.sigmoid(g)
    padded = jnp.concatenate([prefix.astype(u.dtype), u], axis=1)
    y = lax.conv_general_dilated(
        padded, w_dw[:, None, :].astype(u.dtype), window_strides=(1,), padding='VALID',
        dimension_numbers=('NWC', 'WIO', 'NWC'), feature_group_count=CONV_CH) + b_dw
    y = layernorm(y, cln_g, cln_b)
    y = y * jax.nn.sigmoid(y)
    return y, padded[:, -(CONV_WIDTH - 1):]


def merge_and_ffn(x, att, conv, g_att, g_conv, w_out, ln_ffn, w_gate, w_up, w_down):
    mixed = jnp.concatenate([rmsnorm(att, g_att), rmsnorm(conv, g_conv)], axis=-1)
    x = x + jnp.einsum('bte,ed->btd', mixed, w_out)
    h = rmsnorm(x, ln_ffn)
    hidden = jax.nn.silu(jnp.einsum('btd,df->btf', h, w_gate)) * jnp.einsum('btd,df->btf', h, w_up)
    return x + jnp.einsum('btf,fd->btd', hidden, w_down)


def setup_inputs(seed: int = 0) -> dict:
    key = jax.random.key(seed)
    ks = jax.random.split(key, 24)
    f32 = jnp.float32
    nrm = lambda k, shape: jax.random.normal(k, shape, f32)
    return {
        'x_prompt': nrm(ks[0], (BATCH, SEQ, D_MODEL)),
        'x_sample': nrm(ks[1], (DEC_BATCH, DEC_SEQ, D_MODEL)),
        'cache_k': nrm(ks[2], (DEPTH, DEC_BATCH, PAST_LEN, N_ATT_HEADS, HEAD_DIM)),
        'cache_v': nrm(ks[3], (DEPTH, DEC_BATCH, PAST_LEN, N_ATT_HEADS, HEAD_DIM)),
        'cache_logf': jax.nn.log_sigmoid(3.0 + nrm(ks[4], (DEPTH, DEC_BATCH, PAST_LEN, N_ATT_HEADS))),
        'state_conv': 0.5 * nrm(ks[5], (DEPTH, DEC_BATCH, CONV_WIDTH - 1, CONV_CH)),
        'ln_mix': 1.0 + 0.02 * nrm(ks[6], (DEPTH, D_MODEL)),
        'w_in': nrm(ks[7], (DEPTH, D_MODEL, D_IN)) * D_MODEL ** -0.5,
        'b_f': 3.0 + 0.5 * nrm(ks[8], (DEPTH, N_ATT_HEADS)),
        'w_dw': nrm(ks[9], (DEPTH, CONV_WIDTH, CONV_CH)) * CONV_WIDTH ** -0.5,
        'b_dw': 0.02 * nrm(ks[10], (DEPTH, CONV_CH)),
        'cln_g': 1.0 + 0.02 * nrm(ks[11], (DEPTH, CONV_CH)),
        'cln_b': 0.02 * nrm(ks[12], (DEPTH, CONV_CH)),
        'g_att': 1.0 + 0.02 * nrm(ks[13], (DEPTH, ATT_WIDTH)),
        'g_conv': 1.0 + 0.02 * nrm(ks[14], (DEPTH, CONV_CH)),
        'w_out': nrm(ks[15], (DEPTH, MIX_WIDTH, D_MODEL)) * MIX_WIDTH ** -0.5,
        'ln_ffn': 1.0 + 0.02 * nrm(ks[16], (DEPTH, D_MODEL)),
        'w_gate': nrm(ks[17], (DEPTH, D_MODEL, D_FF)) * D_MODEL ** -0.5,
        'w_up': nrm(ks[18], (DEPTH, D_MODEL, D_FF)) * D_MODEL ** -0.5,
        'w_down': nrm(ks[19], (DEPTH, D_FF, D_MODEL)) * D_FF ** -0.5,
        'g_final': 1.0 + 0.02 * nrm(ks[20], (D_MODEL,)),
    }


def reference(x_prompt, x_sample, cache_k, cache_v, cache_logf, state_conv, ln_mix, w_in, b_f,
              w_dw, b_dw, cln_g, cln_b, g_att, g_conv, w_out, ln_ffn, w_gate, w_up, w_down, g_final):
    xp, xs = x_prompt, x_sample
    past = cache_k.shape[2]
    kp, vp, fp, cp = [], [], [], []
    ksm, vsm, fsm, csm = [], [], [], []
    for l in range(DEPTH):
        q, k, v, logf, glu_in = mixer_inputs(xp, ln_mix[l], w_in[l], b_f[l])
        cum = jnp.cumsum(logf, axis=1)
        att = fox_attention(q, k, v, cum, cum, 0)
        zeros = jnp.zeros((xp.shape[0], CONV_WIDTH - 1, CONV_CH), xp.dtype)
        conv, conv_st = conformer_conv(glu_in, zeros, w_dw[l], b_dw[l], cln_g[l], cln_b[l])
        xp = merge_and_ffn(xp, att, conv, g_att[l], g_conv[l], w_out[l], ln_ffn[l],
                           w_gate[l], w_up[l], w_down[l])
        kp.append(k); vp.append(v); fp.append(logf.astype(xp.dtype)); cp.append(conv_st)

        q, k, v, logf, glu_in = mixer_inputs(xs, ln_mix[l], w_in[l], b_f[l])
        k_all = jnp.concatenate([cache_k[l].astype(k.dtype), k], axis=1)
        v_all = jnp.concatenate([cache_v[l].astype(v.dtype), v], axis=1)
        cum = jnp.cumsum(jnp.concatenate([cache_logf[l].astype(jnp.float32), logf], axis=1), axis=1)
        att = fox_attention(q, k_all, v_all, cum[:, past:], cum, past)
        conv, conv_st = conformer_conv(glu_in, state_conv[l], w_dw[l], b_dw[l], cln_g[l], cln_b[l])
        xs = merge_and_ffn(xs, att, conv, g_att[l], g_conv[l], w_out[l], ln_ffn[l],
                           w_gate[l], w_up[l], w_down[l])
        ksm.append(k); vsm.append(v); fsm.append(logf.astype(xs.dtype)); csm.append(conv_st)

    y_prompt = rmsnorm(xp, g_final)
    y_sample = rmsnorm(xs, g_final)
    return (y_prompt, y_sample,
            jnp.stack(kp), jnp.stack(vp), jnp.stack(fp), jnp.stack(cp),
            jnp.stack(ksm), jnp.stack(vsm), jnp.stack(fsm), jnp.stack(csm))
```

```python
import functools

import jax
import jax.numpy as jnp
from jax import lax
from jax.experimental import pallas as pl
from jax.experimental.pallas import tpu as pltpu

HEAD_DIM = 128
CONV_WIDTH = 31
HALO = 32
EPS = 1e-6
NEG = -1e30
LANES = 128
VMEM_LIMIT = 56 * 1024 * 1024

BF16 = jnp.bfloat16
F32 = jnp.float32


def _tile(dim, pref, mult):
    t = min(pref, dim)
    t -= t % mult
    while t >= mult:
        if dim % t == 0:
            return t
        t -= mult
    return dim


def _params(*sem):
    return pltpu.CompilerParams(dimension_semantics=sem, vmem_limit_bytes=VMEM_LIMIT)


def _dot(a, b):
    return jnp.dot(a, b, preferred_element_type=F32)


def _log_sigmoid(x):
    return jnp.minimum(x, 0.0) - jnp.log1p(jnp.exp(-jnp.abs(x)))


def _rms_scale(x):
    return lax.rsqrt(jnp.mean(x * x, axis=-1, keepdims=True) + EPS)


def _norm_gate_kernel(x_ref, g_ref, wf_ref, bf_ref, h_ref, lf_ref):
    x = x_ref[...]
    h = (x * _rms_scale(x) * g_ref[...]).astype(BF16)
    h_ref[...] = h
    f = _dot(h, wf_ref[...]) + bf_ref[...]
    lf_ref[...] = _log_sigmoid(f)[:, :lf_ref.shape[1]]


def norm_gate(x, g, wf, bf, n_heads):
    m, d = x.shape
    tm = _tile(m, 256, 8)
    return pl.pallas_call(
        _norm_gate_kernel,
        grid=(m // tm,),
        in_specs=[pl.BlockSpec((tm, d), lambda i: (i, 0)),
                  pl.BlockSpec((1, d), lambda i: (0, 0)),
                  pl.BlockSpec((d, LANES), lambda i: (0, 0)),
                  pl.BlockSpec((1, LANES), lambda i: (0, 0))],
        out_specs=[pl.BlockSpec((tm, d), lambda i: (i, 0)),
                   pl.BlockSpec((tm, n_heads), lambda i: (i, 0))],
        out_shape=[jax.ShapeDtypeStruct((m, d), BF16),
                   jax.ShapeDtypeStruct((m, n_heads), F32)],
        compiler_params=_params("parallel"),
        name="norm_gate",
    )(x, g, wf, bf)


def _rmsnorm_kernel(x_ref, g_ref, o_ref):
    x = x_ref[...]
    o_ref[...] = (x * _rms_scale(x) * g_ref[...]).astype(o_ref.dtype)


def rmsnorm(x, g, out_dtype):
    m, d = x.shape
    tm = _tile(m, 256, 8)
    return pl.pallas_call(
        _rmsnorm_kernel,
        grid=(m // tm,),
        in_specs=[pl.BlockSpec((tm, d), lambda i: (i, 0)),
                  pl.BlockSpec((1, d), lambda i: (0, 0))],
        out_specs=pl.BlockSpec((tm, d), lambda i: (i, 0)),
        out_shape=jax.ShapeDtypeStruct((m, d), out_dtype),
        compiler_params=_params("parallel"),
        name="rmsnorm",
    )(x, g)


def _cumsum_kernel(x_ref, off_ref, o_ref, *, rows_per_seq, n_heads):
    x = x_ref[...]
    tb = x.shape[0]
    head_bits = n_heads.bit_length() - 1
    seq_bits = rows_per_seq.bit_length() - 1
    li = lax.broadcasted_iota(jnp.int32, (LANES, LANES), 0)
    lj = lax.broadcasted_iota(jnp.int32, (LANES, LANES), 1)
    same_head = (li & (n_heads - 1)) == (lj & (n_heads - 1))
    upper = (same_head & ((li >> head_bits) <= (lj >> head_bits))).astype(F32)
    every = same_head.astype(F32)
    hi = lax.Precision.HIGHEST
    in_row = jnp.dot(x, upper, precision=hi, preferred_element_type=F32)
    row_tot = jnp.dot(x, every, precision=hi, preferred_element_type=F32)
    ri = lax.broadcasted_iota(jnp.int32, (tb, tb), 0)
    rj = lax.broadcasted_iota(jnp.int32, (tb, tb), 1)
    earlier = (((ri >> seq_bits) == (rj >> seq_bits)) & (rj < ri)).astype(F32)
    before = jnp.dot(earlier, row_tot, precision=hi, preferred_element_type=F32)
    o_ref[...] = in_row + before + off_ref[...]


def cumsum_time(logf, offset=None):
    b, t, h = logf.shape
    per_row = LANES // h
    rows = t // per_row
    assert h & (h - 1) == 0 and rows & (rows - 1) == 0 and rows * per_row == t
    x = logf.reshape(b * rows, LANES)
    if offset is None:
        off = jnp.zeros((b * rows, LANES), F32)
    else:
        off = jnp.broadcast_to(offset[:, None, None, :], (b, rows, per_row, h)).reshape(b * rows, LANES)
    tb = rows * _tile(b, max(1, 512 // rows), 1)
    out = pl.pallas_call(
        functools.partial(_cumsum_kernel, rows_per_seq=rows, n_heads=h),
        grid=(b * rows // tb,),
        in_specs=[pl.BlockSpec((tb, LANES), lambda i: (i, 0)),
                  pl.BlockSpec((tb, LANES), lambda i: (i, 0))],
        out_specs=pl.BlockSpec((tb, LANES), lambda i: (i, 0)),
        out_shape=jax.ShapeDtypeStruct((b * rows, LANES), F32),
        compiler_params=_params("parallel"),
        name="cumsum_time",
    )(x, off)
    return out.reshape(b, t, h)


def _matmul_kernel(x_ref, w_ref, o_ref):
    o_ref[...] = _dot(x_ref[...], w_ref[...]).astype(o_ref.dtype)


def matmul(x, w, out_dtype, tm=1024, tn=1024):
    m, k = x.shape
    n = w.shape[1]
    tm, tn = _tile(m, tm, 8), _tile(n, tn, LANES)
    return pl.pallas_call(
        _matmul_kernel,
        grid=(m // tm, n // tn),
        in_specs=[pl.BlockSpec((tm, k), lambda i, j: (i, 0)),
                  pl.BlockSpec((k, tn), lambda i, j: (0, j))],
        out_specs=pl.BlockSpec((tm, tn), lambda i, j: (i, j)),
        out_shape=jax.ShapeDtypeStruct((m, n), out_dtype),
        compiler_params=_params("parallel", "parallel"),
        name="matmul",
    )(x, w)


def _out_proj_kernel(a_ref, c_ref, wa_ref, wc_ref, x_ref, o_ref):
    o_ref[...] = x_ref[...] + _dot(a_ref[...], wa_ref[...]) + _dot(c_ref[...], wc_ref[...])


def out_proj(att, conv, w_out, x, tm=1024, tn=1024):
    m, ka = att.shape
    kc = conv.shape[1]
    n = w_out.shape[1]
    assert ka == kc
    tm, tn = _tile(m, tm, 8), _tile(n, tn, LANES)
    return pl.pallas_call(
        _out_proj_kernel,
        grid=(m // tm, n // tn),
        in_specs=[pl.BlockSpec((tm, ka), lambda i, j: (i, 0)),
                  pl.BlockSpec((tm, kc), lambda i, j: (i, 0)),
                  pl.BlockSpec((ka, tn), lambda i, j: (0, j)),
                  pl.BlockSpec((kc, tn), lambda i, j: (1, j)),
                  pl.BlockSpec((tm, tn), lambda i, j: (i, j))],
        out_specs=pl.BlockSpec((tm, tn), lambda i, j: (i, j)),
        out_shape=jax.ShapeDtypeStruct((m, n), F32),
        compiler_params=_params("parallel", "parallel"),
        name="out_proj",
    )(att, conv, w_out, w_out, x)


def _ffn_up_kernel(h_ref, wg_ref, wu_ref, o_ref):
    h = h_ref[...]
    g = _dot(h, wg_ref[...])
    u = _dot(h, wu_ref[...])
    o_ref[...] = (g * jax.nn.sigmoid(g) * u).astype(o_ref.dtype)


def ffn_up(h, wg, wu, tm=1024, tn=512):
    m, k = h.shape
    n = wg.shape[1]
    tm, tn = _tile(m, tm, 8), _tile(n, tn, LANES)
    return pl.pallas_call(
        _ffn_up_kernel,
        grid=(m // tm, n // tn),
        in_specs=[pl.BlockSpec((tm, k), lambda i, j: (i, 0)),
                  pl.BlockSpec((k, tn), lambda i, j: (0, j)),
                  pl.BlockSpec((k, tn), lambda i, j: (0, j))],
        out_specs=pl.BlockSpec((tm, tn), lambda i, j: (i, j)),
        out_shape=jax.ShapeDtypeStruct((m, n), BF16),
        compiler_params=_params("parallel", "parallel"),
        name="ffn_up",
    )(h, wg, wu)


def _ffn_down_kernel(h_ref, w_ref, x_ref, o_ref):
    o_ref[...] = x_ref[...] + _dot(h_ref[...], w_ref[...])


def ffn_down(hidden, wd, x, tm=512, tn=512):
    m, k = hidden.shape
    n = wd.shape[1]
    tm, tn = _tile(m, tm, 8), _tile(n, tn, LANES)
    return pl.pallas_call(
        _ffn_down_kernel,
        grid=(m // tm, n // tn),
        in_specs=[pl.BlockSpec((tm, k), lambda i, j: (i, 0)),
                  pl.BlockSpec((k, tn), lambda i, j: (0, j)),
                  pl.BlockSpec((tm, tn), lambda i, j: (i, j))],
        out_specs=pl.BlockSpec((tm, tn), lambda i, j: (i, j)),
        out_shape=jax.ShapeDtypeStruct((m, n), F32),
        compiler_params=_params("parallel", "parallel"),
        name="ffn_down",
    )(hidden, wd, x)


def _softmax_step(s, v, m_prev, l_prev, acc_prev):
    m_new = jnp.maximum(m_prev, jnp.max(s, axis=-1, keepdims=True))
    alpha = jnp.exp(m_prev - m_new)
    p = jnp.exp(s - m_new)
    l_new = alpha * l_prev + jnp.sum(p, axis=-1, keepdims=True)
    acc_new = alpha * acc_prev + _dot(p.astype(BF16), v)
    return m_new, l_new, acc_new


def _qk(q, k):
    return lax.dot_general(q, k, (((1,), (1,)), ((), ())), preferred_element_type=F32)


def _causal(s):
    row = lax.broadcasted_iota(jnp.int32, s.shape, 0)
    col = lax.broadcasted_iota(jnp.int32, s.shape, 1)
    return jnp.where(col <= row, s, NEG)


def _flash_kernel(q_ref, k_ref, v_ref, cq_ref, ck_ref, o_ref, q_s, dq_s, m_s, l_s, acc_s, *, scale):
    h = pl.program_id(1)
    qi = pl.program_id(2)
    ki = pl.program_id(3)

    @pl.when(ki == 0)
    def _():
        q_s[...] = (q_ref[...] * scale).astype(BF16)
        cq = cq_ref[...]
        lane = lax.broadcasted_iota(jnp.int32, cq.shape, 1)
        dq_s[...] = jnp.sum(jnp.where(lane == h, cq, 0.0), axis=-1, keepdims=True)
        m_s[...] = jnp.full(m_s.shape, NEG, F32)
        l_s[...] = jnp.zeros(l_s.shape, F32)
        acc_s[...] = jnp.zeros(acc_s.shape, F32)

    def step(masked):
        s = _qk(q_s[...], k_ref[...].astype(BF16)) + dq_s[...] - ck_ref[pl.ds(h, 1), :]
        if masked:
            s = _causal(s)
        m_s[...], l_s[...], acc_s[...] = _softmax_step(
            s, v_ref[...].astype(BF16), m_s[...], l_s[...], acc_s[...])

    @pl.when(ki < qi)
    def _():
        step(False)

    @pl.when(ki == qi)
    def _():
        step(True)
        o_ref[...] = acc_s[...] / l_s[...]


def flash_attention(z, cum, cum_t, n_batch, seq, n_heads, tq=512):
    tq = _tile(seq, tq, LANES)
    nq = seq // tq
    hd = HEAD_DIM
    kernel = functools.partial(_flash_kernel, scale=hd ** -0.5)
    return pl.pallas_call(
        kernel,
        grid=(n_batch, n_heads, nq, nq),
        in_specs=[pl.BlockSpec((tq, hd), lambda b, h, qi, ki: (b * nq + qi, h)),
                  pl.BlockSpec((tq, hd), lambda b, h, qi, ki: (b * nq + jnp.minimum(ki, qi), n_heads + h)),
                  pl.BlockSpec((tq, hd), lambda b, h, qi, ki: (b * nq + jnp.minimum(ki, qi), 2 * n_heads + h)),
                  pl.BlockSpec((tq, n_heads), lambda b, h, qi, ki: (b * nq + qi, 0)),
                  pl.BlockSpec((None, n_heads, tq), lambda b, h, qi, ki: (b, 0, jnp.minimum(ki, qi)))],
        out_specs=pl.BlockSpec((tq, hd), lambda b, h, qi, ki: (b * nq + qi, h)),
        out_shape=jax.ShapeDtypeStruct((n_batch * seq, n_heads * hd), F32),
        scratch_shapes=[pltpu.VMEM((tq, hd), BF16),
                        pltpu.VMEM((tq, 1), F32),
                        pltpu.VMEM((tq, 1), F32),
                        pltpu.VMEM((tq, 1), F32),
                        pltpu.VMEM((tq, hd), F32)],
        compiler_params=_params("parallel", "parallel", "parallel", "arbitrary"),
        name="flash_attention",
    )(z, z, z, cum, cum_t)


def _decode_kernel(q_ref, kn_ref, vn_ref, kc_ref, vc_ref, cq_ref, ckc_ref, ckn_ref, g_ref, o_ref,
                   q_s, m_s, l_s, acc_s, *, scale, n_heads):
    kb = pl.program_id(1)
    hd = HEAD_DIM

    @pl.when(kb == 0)
    def _():
        q_s[...] = (q_ref[...] * scale).astype(BF16)
        m_s[...] = jnp.full(m_s.shape, NEG, F32)
        l_s[...] = jnp.zeros(l_s.shape, F32)
        acc_s[...] = jnp.zeros(acc_s.shape, F32)

    def update(k_ref, v_ref, ck_ref, masked):
        for h in range(n_heads):
            cols = slice(h * hd, (h + 1) * hd)
            s = _qk(q_s[:, cols], k_ref[:, cols].astype(BF16)) + cq_ref[:, h:h + 1] - ck_ref[h:h + 1, :]
            if masked:
                s = _causal(s)
            m_s[:, h:h + 1], l_s[:, h:h + 1], acc_s[:, cols] = _softmax_step(
                s, v_ref[:, cols].astype(BF16), m_s[:, h:h + 1], l_s[:, h:h + 1], acc_s[:, cols])

    update(kc_ref, vc_ref, ckc_ref, False)

    @pl.when(kb == pl.num_programs(1) - 1)
    def _():
        update(kn_ref, vn_ref, ckn_ref, True)
        for h in range(n_heads):
            cols = slice(h * hd, (h + 1) * hd)
            acc_s[:, cols] = acc_s[:, cols] / l_s[:, h:h + 1]
        att = acc_s[...]
        o_ref[...] = (att * _rms_scale(att) * g_ref[...]).astype(o_ref.dtype)


def decode_attention(z, row0, cache_k, cache_v, layer, cum_new, cum_cache_t, cum_new_t, g_att,
                     n_batch, t_new, n_heads, tk=1024):
    width = n_heads * HEAD_DIM
    past = cache_k.shape[2]
    tk = _tile(past, tk, LANES)
    r0 = row0 // t_new
    kernel = functools.partial(_decode_kernel, scale=HEAD_DIM ** -0.5, n_heads=n_heads)
    return pl.pallas_call(
        kernel,
        grid=(n_batch, past // tk),
        in_specs=[pl.BlockSpec((t_new, width), lambda b, kb: (r0 + b, 0)),
                  pl.BlockSpec((t_new, width), lambda b, kb: (r0 + b, 1)),
                  pl.BlockSpec((t_new, width), lambda b, kb: (r0 + b, 2)),
                  pl.BlockSpec((None, None, tk, width), lambda b, kb: (layer, b, kb, 0)),
                  pl.BlockSpec((None, None, tk, width), lambda b, kb: (layer, b, kb, 0)),
                  pl.BlockSpec((t_new, n_heads), lambda b, kb: (b, 0)),
                  pl.BlockSpec((None, None, n_heads, tk), lambda b, kb: (layer, b, 0, kb)),
                  pl.BlockSpec((None, n_heads, t_new), lambda b, kb: (b, 0, 0)),
                  pl.BlockSpec((1, width), lambda b, kb: (0, 0))],
        out_specs=pl.BlockSpec((t_new, width), lambda b, kb: (b, 0)),
        out_shape=jax.ShapeDtypeStruct((n_batch * t_new, width), BF16),
        scratch_shapes=[pltpu.VMEM((t_new, width), BF16),
                        pltpu.VMEM((t_new, LANES), F32),
                        pltpu.VMEM((t_new, LANES), F32),
                        pltpu.VMEM((t_new, width), F32)],
        compiler_params=_params("parallel", "arbitrary"),
        name="decode_attention",
    )(z, z, z, cache_k, cache_v, cum_new, cum_cache_t, cum_new_t, g_att)


def _conv_kernel(a_ref, g_ref, pre_ref, w_ref, b_ref, lng_ref, lnb_ref, gc_ref, o_ref, st_ref,
                 u_s, y_s, *, tt, rw, cw):
    t = pl.program_id(1)
    width = u_s.shape[1]

    @pl.when(t == 0)
    def _():
        u_s[0:HALO, :] = pre_ref[...]

    @pl.when(t > 0)
    def _():
        u_s[0:HALO, :] = u_s[tt:tt + HALO, :]

    a = a_ref[...]
    u_s[HALO:HALO + tt, :] = a * jax.nn.sigmoid(g_ref[...])

    first = HALO - (CONV_WIDTH - 1)

    for r0 in range(0, tt, rw):
        for c in range(width // cw):
            cols = slice(c * cw, (c + 1) * cw)
            acc = jnp.broadcast_to(b_ref[:, cols], (rw, cw))
            for j in range(CONV_WIDTH):
                acc = acc + u_s[r0 + first + j:r0 + first + j + rw, cols] * w_ref[j:j + 1, cols]
            y_s[r0:r0 + rw, cols] = acc

    y = y_s[...]
    mu = jnp.mean(y, axis=-1, keepdims=True)
    yc = y - mu
    y = yc * lax.rsqrt(jnp.mean(yc * yc, axis=-1, keepdims=True) + EPS) * lng_ref[...] + lnb_ref[...]
    y = y * jax.nn.sigmoid(y)
    o_ref[...] = (y * _rms_scale(y) * gc_ref[...]).astype(o_ref.dtype)

    @pl.when(t == pl.num_programs(1) - 1)
    def _():
        st_ref[...] = u_s[tt + first:tt + HALO, :]


def conv_branch(z, row0, col0, prefix, w_dw, b_dw, cln_g, cln_b, g_conv, n_seq, seq, tt=256):
    width = w_dw.shape[1]
    tt = _tile(seq, tt, 8)
    nt = seq // tt
    r0 = row0 // tt
    rw = _tile(tt, 32, 8)
    kernel = functools.partial(_conv_kernel, tt=tt, rw=rw, cw=256)
    vec = lambda n: pl.BlockSpec((n, width), lambda s, t: (0, 0))
    return pl.pallas_call(
        kernel,
        grid=(n_seq, nt),
        in_specs=[pl.BlockSpec((tt, width), lambda s, t: (r0 + s * nt + t, col0)),
                  pl.BlockSpec((tt, width), lambda s, t: (r0 + s * nt + t, col0 + 1)),
                  pl.BlockSpec((None, HALO, width), lambda s, t: (s, 0, 0)),
                  vec(CONV_WIDTH), vec(1), vec(1), vec(1), vec(1)],
        out_specs=[pl.BlockSpec((tt, width), lambda s, t: (s * nt + t, 0)),
                   pl.BlockSpec((None, CONV_WIDTH - 1, width), lambda s, t: (s, 0, 0))],
        out_shape=[jax.ShapeDtypeStruct((n_seq * seq, width), BF16),
                   jax.ShapeDtypeStruct((n_seq, CONV_WIDTH - 1, width), F32)],
        scratch_shapes=[pltpu.VMEM((tt + HALO, width), F32),
                        pltpu.VMEM((tt, width), F32)],
        compiler_params=_params("parallel", "arbitrary"),
        name="conv_branch",
    )(z, z, prefix, w_dw, b_dw, cln_g, cln_b, g_conv)


def kernel(x_prompt, x_sample, cache_k, cache_v, cache_logf, state_conv, ln_mix, w_in, b_f,
           w_dw, b_dw, cln_g, cln_b, g_att, g_conv, w_out, ln_ffn, w_gate, w_up, w_down, g_final):
    n_b, seq, d = x_prompt.shape
    n_s, t_new, _ = x_sample.shape
    depth, _, past, n_heads, hd = cache_k.shape
    assert hd == HEAD_DIM
    att_w = n_heads * hd
    conv_w = w_dw.shape[2]
    assert conv_w == att_w and w_in.shape[2] == 3 * att_w + n_heads + 2 * conv_w
    d_ff = w_gate.shape[2]
    d_ff_pad = -(-d_ff // 1024) * 1024
    mp = n_b * seq
    ms = n_s * t_new
    row = lambda v: v.reshape(1, -1)

    x = jnp.concatenate([x_prompt.reshape(mp, d), x_sample.reshape(ms, d)], axis=0)
    ck = cache_k.reshape(depth, n_s, past, att_w)
    cv = cache_v.reshape(depth, n_s, past, att_w)
    cum_cache = cumsum_time(cache_logf.reshape(depth * n_s, past, n_heads)).reshape(depth, n_s, past, n_heads)
    cum_cache_t = jnp.swapaxes(cum_cache, 2, 3)
    zero_prefix = jnp.zeros((n_b, HALO, conv_w), F32)

    outs = {k: [] for k in ("kp", "vp", "fp", "cp", "ks", "vs", "fs", "cs")}
    for l in range(depth):
        gate0 = 3 * att_w
        w_main = jnp.concatenate([w_in[l][:, :gate0], w_in[l][:, gate0 + n_heads:]], axis=1).astype(BF16)
        w_f = jnp.pad(w_in[l][:, gate0:gate0 + n_heads], ((0, 0), (0, LANES - n_heads))).astype(BF16)
        b_fp = jnp.pad(b_f[l], (0, LANES - n_heads)).reshape(1, LANES)

        h, logf = norm_gate(x, row(ln_mix[l]), w_f, b_fp, n_heads)
        z = matmul(h, w_main, F32)

        logf_p = logf[:mp].reshape(n_b, seq, n_heads)
        logf_s = logf[mp:].reshape(n_s, t_new, n_heads)
        cum_p = cumsum_time(logf_p)
        cum_s = cumsum_time(logf_s, offset=cum_cache[l][:, -1, :])

        att_p = flash_attention(z, cum_p.reshape(mp, n_heads), jnp.swapaxes(cum_p, 1, 2), n_b, seq, n_heads)
        att_p = rmsnorm(att_p, row(g_att[l]), BF16)
        att_s = decode_attention(z, mp, ck, cv, l, cum_s.reshape(ms, n_heads), cum_cache_t,
                                 jnp.swapaxes(cum_s, 1, 2), row(g_att[l]), n_s, t_new, n_heads)

        conv_args = (w_dw[l], row(b_dw[l]), row(cln_g[l]), row(cln_b[l]), row(g_conv[l]))
        conv_p, st_p = conv_branch(z, 0, 3, zero_prefix, *conv_args, n_b, seq)
        prefix_s = jnp.pad(state_conv[l], ((0, 0), (HALO - (CONV_WIDTH - 1), 0), (0, 0)))
        conv_s, st_s = conv_branch(z, mp, 3, prefix_s, *conv_args, n_s, t_new)

        att = jnp.concatenate([att_p, att_s], axis=0)
        conv = jnp.concatenate([conv_p, conv_s], axis=0)
        x = out_proj(att, conv, w_out[l].astype(BF16), x)

        h2 = rmsnorm(x, row(ln_ffn[l]), BF16)
        pad_ff = ((0, 0), (0, d_ff_pad - d_ff))
        hidden = ffn_up(h2, jnp.pad(w_gate[l], pad_ff).astype(BF16), jnp.pad(w_up[l], pad_ff).astype(BF16))
        x = ffn_down(hidden, jnp.pad(w_down[l], ((0, d_ff_pad - d_ff), (0, 0))).astype(BF16), x)

        k_all = z[:, att_w:2 * att_w]
        v_all = z[:, 2 * att_w:3 * att_w]
        outs["kp"].append(k_all[:mp].reshape(n_b, seq, n_heads, hd))
        outs["vp"].append(v_all[:mp].reshape(n_b, seq, n_heads, hd))
        outs["fp"].append(logf_p)
        outs["cp"].append(st_p)
        outs["ks"].append(k_all[mp:].reshape(n_s, t_new, n_heads, hd))
        outs["vs"].append(v_all[mp:].reshape(n_s, t_new, n_heads, hd))
        outs["fs"].append(logf_s)
        outs["cs"].append(st_s)

    y = rmsnorm(x, row(g_final), F32)
    stk = lambda k: jnp.stack(outs[k])
    return (y[:mp].reshape(n_b, seq, d), y[mp:].reshape(n_s, t_new, d),
            stk("kp"), stk("vp"), stk("fp"), stk("cp"),
            stk("ks"), stk("vs"), stk("fs"), stk("cs"))
```

```python
import functools
import math

import jax
import jax.numpy as jnp
from jax import lax
from jax.experimental import pallas as pl
from jax.experimental.pallas import tpu as pltpu

HEAD_DIM = 128
CONV_WIDTH = 31
HALO = 32
EPS = 1e-6
NEG = -1e30
LOG2E = math.log2(math.e)
LANES = 128
SUBLANES = 8
VMEM_LIMIT = 56 * 1024 * 1024

BF16 = jnp.bfloat16
F32 = jnp.float32


def _tile(dim, pref, mult):
    t = min(pref, dim)
    t -= t % mult
    while t >= mult:
        if dim % t == 0:
            return t
        t -= mult
    return dim


def _params(*sem):
    return pltpu.CompilerParams(dimension_semantics=sem, vmem_limit_bytes=VMEM_LIMIT)


def _dot(a, b):
    return jnp.dot(a, b, preferred_element_type=F32)


def _dot_nt(a, b):
    return lax.dot_general(a, b, (((1,), (1,)), ((), ())), preferred_element_type=F32)


def _log_sigmoid(x):
    return jnp.minimum(x, 0.0) - jnp.log1p(jnp.exp(-jnp.abs(x)))


def _rms_scale(x):
    return lax.rsqrt(jnp.mean(x * x, axis=-1, keepdims=True) + EPS)


def _norm_gate_kernel(x_ref, g_ref, wf_ref, bf_ref, h_ref, lf_ref):
    x = x_ref[...]
    h = (x * _rms_scale(x) * g_ref[...]).astype(BF16)
    h_ref[...] = h
    f = _dot(h, wf_ref[...]) + bf_ref[...]
    lf_ref[...] = _log_sigmoid(f)[:, :lf_ref.shape[1]]


def norm_gate(x, g, wf, bf, n_heads):
    m, d = x.shape
    tm = _tile(m, 256, 8)
    return pl.pallas_call(
        _norm_gate_kernel,
        grid=(m // tm,),
        in_specs=[pl.BlockSpec((tm, d), lambda i: (i, 0)),
                  pl.BlockSpec((1, d), lambda i: (0, 0)),
                  pl.BlockSpec((d, LANES), lambda i: (0, 0)),
                  pl.BlockSpec((1, LANES), lambda i: (0, 0))],
        out_specs=[pl.BlockSpec((tm, d), lambda i: (i, 0)),
                   pl.BlockSpec((tm, n_heads), lambda i: (i, 0))],
        out_shape=[jax.ShapeDtypeStruct((m, d), BF16),
                   jax.ShapeDtypeStruct((m, n_heads), F32)],
        compiler_params=_params("parallel"),
        name="norm_gate",
    )(x, g, wf, bf)


def _rmsnorm_kernel(x_ref, g_ref, o_ref):
    x = x_ref[...]
    o_ref[...] = (x * _rms_scale(x) * g_ref[...]).astype(o_ref.dtype)


def rmsnorm(x, g, out_dtype, row0=0, rows=None):
    d = x.shape[1]
    rows = x.shape[0] if rows is None else rows
    tm = _tile(math.gcd(rows, row0) if row0 else rows, 256, 8)
    r0 = row0 // tm
    return pl.pallas_call(
        _rmsnorm_kernel,
        grid=(rows // tm,),
        in_specs=[pl.BlockSpec((tm, d), lambda i: (r0 + i, 0)),
                  pl.BlockSpec((1, d), lambda i: (0, 0))],
        out_specs=pl.BlockSpec((tm, d), lambda i: (i, 0)),
        out_shape=jax.ShapeDtypeStruct((rows, d), out_dtype),
        compiler_params=_params("parallel"),
        name="rmsnorm",
    )(x, g)


def _cumsum_kernel(x_ref, off_ref, o_ref, *, rows_per_seq, n_heads):
    x = x_ref[...]
    tb = x.shape[0]
    head_bits = n_heads.bit_length() - 1
    seq_bits = rows_per_seq.bit_length() - 1
    li = lax.broadcasted_iota(jnp.int32, (LANES, LANES), 0)
    lj = lax.broadcasted_iota(jnp.int32, (LANES, LANES), 1)
    same_head = (li & (n_heads - 1)) == (lj & (n_heads - 1))
    upper = (same_head & ((li >> head_bits) <= (lj >> head_bits))).astype(F32)
    every = same_head.astype(F32)
    hi = lax.Precision.HIGHEST
    in_row = jnp.dot(x, upper, precision=hi, preferred_element_type=F32)
    row_tot = jnp.dot(x, every, precision=hi, preferred_element_type=F32)
    ri = lax.broadcasted_iota(jnp.int32, (tb, tb), 0)
    rj = lax.broadcasted_iota(jnp.int32, (tb, tb), 1)
    earlier = (((ri >> seq_bits) == (rj >> seq_bits)) & (rj < ri)).astype(F32)
    before = jnp.dot(earlier, row_tot, precision=hi, preferred_element_type=F32)
    o_ref[...] = in_row + before + off_ref[...]


def cumsum_time(logf, offset=None):
    b, t, h = logf.shape
    per_row = LANES // h
    rows = t // per_row
    assert h & (h - 1) == 0 and rows & (rows - 1) == 0 and rows * per_row == t
    x = logf.reshape(b * rows, LANES)
    if offset is None:
        off = jnp.zeros((b * rows, LANES), F32)
    else:
        off = jnp.broadcast_to(offset[:, None, None, :], (b, rows, per_row, h)).reshape(b * rows, LANES)
    tb = rows * _tile(b, max(1, 512 // rows), 1)
    out = pl.pallas_call(
        functools.partial(_cumsum_kernel, rows_per_seq=rows, n_heads=h),
        grid=(b * rows // tb,),
        in_specs=[pl.BlockSpec((tb, LANES), lambda i: (i, 0)),
                  pl.BlockSpec((tb, LANES), lambda i: (i, 0))],
        out_specs=pl.BlockSpec((tb, LANES), lambda i: (i, 0)),
        out_shape=jax.ShapeDtypeStruct((b * rows, LANES), F32),
        compiler_params=_params("parallel"),
        name="cumsum_time",
    )(x, off)
    return out.reshape(b, t, h)


def _cast_on_first_row_tile(w_refs, wb_refs):
    @pl.when(pl.program_id(1) == 0)
    def _():
        for w_ref, wb_ref in zip(w_refs, wb_refs):
            wb_ref[...] = w_ref[...].astype(BF16)


def _proj_kernel(x_ref, w_ref, o_ref, wb_s, *, out_scale):
    _cast_on_first_row_tile([w_ref], [wb_s])
    acc = _dot(x_ref[...], wb_s[...])
    if out_scale != 1.0:
        acc = acc * out_scale
    o_ref[...] = acc.astype(o_ref.dtype)


def proj(x, w, layer, col0, n, out_dtype, out_scale=1.0, tm=1024, tn=512):
    m, k = x.shape
    tm, tn = _tile(m, tm, 8), _tile(math.gcd(n, col0) if col0 else n, tn, LANES)
    c0 = col0 // tn
    return pl.pallas_call(
        functools.partial(_proj_kernel, out_scale=out_scale),
        grid=(n // tn, m // tm),
        in_specs=[pl.BlockSpec((tm, k), lambda j, i: (i, 0)),
                  pl.BlockSpec((None, k, tn), lambda j, i: (layer, 0, c0 + j))],
        out_specs=pl.BlockSpec((tm, tn), lambda j, i: (i, j)),
        out_shape=jax.ShapeDtypeStruct((m, n), out_dtype),
        scratch_shapes=[pltpu.VMEM((k, tn), BF16)],
        compiler_params=_params("arbitrary", "arbitrary"),
        name="proj",
    )(x, w)


def _proj2_kernel(x_ref, w_ref, o_ref, ob_ref, wb_s):
    _cast_on_first_row_tile([w_ref], [wb_s])
    acc = _dot(x_ref[...], wb_s[...])
    o_ref[...] = acc
    ob_ref[...] = acc.astype(BF16)


def proj2(x, w, layer, col0, n, tm=1024, tn=512):
    m, k = x.shape
    tm, tn = _tile(m, tm, 8), _tile(math.gcd(n, col0) if col0 else n, tn, LANES)
    c0 = col0 // tn
    out = pl.BlockSpec((tm, tn), lambda j, i: (i, j))
    return pl.pallas_call(
        _proj2_kernel,
        grid=(n // tn, m // tm),
        in_specs=[pl.BlockSpec((tm, k), lambda j, i: (i, 0)),
                  pl.BlockSpec((None, k, tn), lambda j, i: (layer, 0, c0 + j))],
        out_specs=[out, out],
        out_shape=[jax.ShapeDtypeStruct((m, n), F32), jax.ShapeDtypeStruct((m, n), BF16)],
        scratch_shapes=[pltpu.VMEM((k, tn), BF16)],
        compiler_params=_params("arbitrary", "arbitrary"),
        name="proj2",
    )(x, w)


def _out_proj_kernel(a_ref, c_ref, wa_ref, wc_ref, x_ref, o_ref, wab_s, wcb_s):
    _cast_on_first_row_tile([wa_ref, wc_ref], [wab_s, wcb_s])
    o_ref[...] = x_ref[...] + _dot(a_ref[...], wab_s[...]) + _dot(c_ref[...], wcb_s[...])


def out_proj(att, conv, w_out, layer, x, tm=1024, tn=512):
    m, ka = att.shape
    kc = conv.shape[1]
    n = w_out.shape[2]
    assert ka == kc
    tm, tn = _tile(m, tm, 8), _tile(n, tn, LANES)
    return pl.pallas_call(
        _out_proj_kernel,
        grid=(n // tn, m // tm),
        in_specs=[pl.BlockSpec((tm, ka), lambda j, i: (i, 0)),
                  pl.BlockSpec((tm, kc), lambda j, i: (i, 0)),
                  pl.BlockSpec((None, ka, tn), lambda j, i: (layer, 0, j)),
                  pl.BlockSpec((None, kc, tn), lambda j, i: (layer, 1, j)),
                  pl.BlockSpec((tm, tn), lambda j, i: (i, j))],
        out_specs=pl.BlockSpec((tm, tn), lambda j, i: (i, j)),
        out_shape=jax.ShapeDtypeStruct((m, n), F32),
        scratch_shapes=[pltpu.VMEM((ka, tn), BF16), pltpu.VMEM((kc, tn), BF16)],
        compiler_params=_params("arbitrary", "arbitrary"),
        name="out_proj",
    )(att, conv, w_out, w_out, x)


def _ffn_up_kernel(h_ref, wg_ref, wu_ref, o_ref, wgb_s, wub_s):
    _cast_on_first_row_tile([wg_ref, wu_ref], [wgb_s, wub_s])
    h = h_ref[...]
    g = _dot(h, wgb_s[...])
    u = _dot(h, wub_s[...])
    o_ref[...] = (g * jax.nn.sigmoid(g) * u).astype(o_ref.dtype)


def ffn_up(h, w_gate, w_up, layer, tm=1024, tn=256):
    m, k = h.shape
    n = w_gate.shape[2]
    tm, tn = _tile(m, tm, 8), _tile(n, tn, LANES)
    wspec = pl.BlockSpec((None, k, tn), lambda j, i: (layer, 0, j))
    return pl.pallas_call(
        _ffn_up_kernel,
        grid=(n // tn, m // tm),
        in_specs=[pl.BlockSpec((tm, k), lambda j, i: (i, 0)), wspec, wspec],
        out_specs=pl.BlockSpec((tm, tn), lambda j, i: (i, j)),
        out_shape=jax.ShapeDtypeStruct((m, n), BF16),
        scratch_shapes=[pltpu.VMEM((k, tn), BF16), pltpu.VMEM((k, tn), BF16)],
        compiler_params=_params("arbitrary", "arbitrary"),
        name="ffn_up",
    )(h, w_gate, w_up)


def _ffn_down_kernel(h_ref, w_ref, x_ref, o_ref):
    o_ref[...] = x_ref[...] + _dot(h_ref[...], w_ref[...])


def ffn_down(hidden, wd, layer, x, tm=512, tn=512):
    m, k = hidden.shape
    n = wd.shape[2]
    tm, tn = _tile(m, tm, 8), _tile(n, tn, LANES)
    return pl.pallas_call(
        _ffn_down_kernel,
        grid=(m // tm, n // tn),
        in_specs=[pl.BlockSpec((tm, k), lambda i, j: (i, 0)),
                  pl.BlockSpec((None, k, tn), lambda i, j: (layer, 0, j)),
                  pl.BlockSpec((tm, tn), lambda i, j: (i, j))],
        out_specs=pl.BlockSpec((tm, tn), lambda i, j: (i, j)),
        out_shape=jax.ShapeDtypeStruct((m, n), F32),
        compiler_params=_params("parallel", "parallel"),
        name="ffn_down",
    )(hidden, wd, x)


def _causal(s):
    row = lax.broadcasted_iota(jnp.int32, s.shape, 0)
    col = lax.broadcasted_iota(jnp.int32, s.shape, 1)
    return jnp.where(col <= row, s, NEG)


def _column(x, h):
    lane = lax.broadcasted_iota(jnp.int32, x.shape, 1)
    return jnp.sum(jnp.where(lane == h, x, 0.0), axis=-1, keepdims=True)


def _split3(x):
    hi = x.astype(BF16).astype(F32)
    r = x - hi
    mid = r.astype(BF16).astype(F32)
    lo = (r - mid).astype(BF16).astype(F32)
    return hi, mid, lo


def _bias_lanes(d, query_side):
    hi, mid, lo = _split3(d)
    lane = lax.broadcasted_iota(jnp.int32, (d.shape[0], LANES), 1)
    one = jnp.where(lane < 6, 1.0, 0.0)
    if query_side:
        v = jnp.where(lane == 0, hi, jnp.where(lane == 1, mid, jnp.where(lane == 2, lo, one)))
    else:
        v = jnp.where(lane == 3, -hi, jnp.where(lane == 4, -mid, jnp.where(lane == 5, -lo, one)))
    return v.astype(BF16)


def _flash_kernel(q_ref, k_ref, v_ref, cq_ref, ck_ref, o_ref, qa_s, ka_s, va_s, m_s, acc_s, *, tq, halves):
    h = pl.program_id(1)
    qi = pl.program_id(2)
    hd = HEAD_DIM

    @pl.when(qi == 0)
    def _():
        ka_s[:, :hd] = k_ref[...]
        ka_s[:, hd:] = _bias_lanes(_column(ck_ref[...], h) * LOG2E, False)
        va_s[:, :hd] = v_ref[...]
        lane = lax.broadcasted_iota(jnp.int32, (va_s.shape[0], LANES), 1)
        va_s[:, hd:] = jnp.where(lane == 0, 1.0, 0.0).astype(BF16)

    qa_s[:, :hd] = q_ref[...]
    qa_s[:, hd:] = _bias_lanes(_column(cq_ref[...], h) * LOG2E, True)
    m_s[...] = jnp.full(m_s.shape, NEG, F32)
    acc_s[...] = jnp.zeros(acc_s.shape, F32)
    th = tq // halves

    def scores(ki):
        k0 = pl.multiple_of(ki * tq, tq)
        ka = ka_s[pl.ds(k0, tq), :]
        return [_dot_nt(qa_s[r * th:(r + 1) * th, :], ka) for r in range(halves)]

    def update(ki, s_halves, masked):
        k0 = pl.multiple_of(ki * tq, tq)
        va = va_s[pl.ds(k0, tq), :]
        for r, s in enumerate(s_halves):
            rows = slice(r * th, (r + 1) * th)
            if masked:
                row = lax.broadcasted_iota(jnp.int32, s.shape, 0) + r * th
                col = lax.broadcasted_iota(jnp.int32, s.shape, 1)
                s = jnp.where(col <= row, s, NEG)
            m_prev = m_s[rows, :]
            m_new = jnp.maximum(m_prev, jnp.max(s, axis=-1, keepdims=True))
            p = jnp.exp2(s - m_new).astype(BF16)
            acc_s[rows, :] = jnp.exp2(m_prev - m_new) * acc_s[rows, :] + _dot(p, va)
            m_s[rows, :] = m_new

    def body(ki, s_halves):
        s_next = scores(ki + 1)
        update(ki, s_halves, False)
        return s_next

    s_last = lax.fori_loop(0, qi, body, scores(0))
    update(qi, s_last, True)
    acc = acc_s[...]
    o_ref[...] = acc[:, :hd] / acc[:, hd:hd + 1]


def flash_attention(q, k, v, cum, n_batch, seq, n_heads, tq=512):
    tq = _tile(seq, tq, LANES)
    nq = seq // tq
    hd = HEAD_DIM
    kernel = functools.partial(_flash_kernel, tq=tq, halves=2)
    return pl.pallas_call(
        kernel,
        grid=(n_batch, n_heads, nq),
        in_specs=[pl.BlockSpec((tq, hd), lambda b, h, qi: (b * nq + qi, h)),
                  pl.BlockSpec((seq, hd), lambda b, h, qi: (b, h)),
                  pl.BlockSpec((seq, hd), lambda b, h, qi: (b, h)),
                  pl.BlockSpec((tq, n_heads), lambda b, h, qi: (b * nq + qi, 0)),
                  pl.BlockSpec((seq, n_heads), lambda b, h, qi: (b, 0))],
        out_specs=pl.BlockSpec((tq, hd), lambda b, h, qi: (b * nq + qi, h)),
        out_shape=jax.ShapeDtypeStruct((n_batch * seq, n_heads * hd), F32),
        scratch_shapes=[pltpu.VMEM((tq, 2 * hd), BF16),
                        pltpu.VMEM((seq, 2 * hd), BF16),
                        pltpu.VMEM((seq, 2 * hd), BF16),
                        pltpu.VMEM((tq, 1), F32),
                        pltpu.VMEM((tq, 2 * hd), F32)],
        compiler_params=_params("parallel", "parallel", "arbitrary"),
        name="flash_attention",
    )(q, k, v, cum, cum)


def _decode_kernel(q_ref, kn_ref, vn_ref, kc_ref, vc_ref, cq_ref, ckc_ref, ckn_ref, g_ref, o_ref,
                   m_s, l_s, acc_s, *, n_heads, t_new, tk):
    kb = pl.program_id(1)
    hd = HEAD_DIM
    hps = n_heads // SUBLANES
    n_rows = tk * hps

    @pl.when(kb == 0)
    def _():
        m_s[...] = jnp.full(m_s.shape, NEG, F32)
        l_s[...] = jnp.zeros(l_s.shape, F32)
        acc_s[...] = jnp.zeros(acc_s.shape, F32)

    def online_update(rows, p, s, v):
        cols = slice(p * hd, (p + 1) * hd)
        m_prev = m_s[p, rows, :]
        m_new = jnp.maximum(m_prev, jnp.max(s, axis=-1, keepdims=True))
        alpha = jnp.exp2(m_prev - m_new)
        e = jnp.exp2(s - m_new)
        l_s[p, rows, :] = alpha * l_s[p, rows, :] + jnp.sum(e, axis=-1, keepdims=True)
        acc_s[rows, cols] = alpha * acc_s[rows, cols] + _dot(e.astype(BF16), v)
        m_s[p, rows, :] = m_new

    if hps > 1:
        shape = (hps * t_new, n_rows)
        row = lax.broadcasted_iota(jnp.int32, shape, 0)
        col = lax.broadcasted_iota(jnp.int32, shape, 1)
        row_head = sum((row >= r * t_new).astype(jnp.int32) for r in range(1, hps))
        own_head = row_head == (col & (hps - 1))

    heads_of = lambda p: [p + SUBLANES * r for r in range(hps)]
    for p in range(SUBLANES):
        q = jnp.concatenate([q_ref[:, h * hd:(h + 1) * hd] for h in heads_of(p)], axis=0)
        dq = jnp.concatenate([cq_ref[:, h:h + 1] for h in heads_of(p)], axis=0)
        k = kc_ref[pl.ds(p, n_rows, stride=SUBLANES), :].astype(BF16)
        v = vc_ref[pl.ds(p, n_rows, stride=SUBLANES), :].astype(BF16)
        s = _dot_nt(q, k) + (dq - ckc_ref[p:p + 1, :])
        if hps > 1:
            s = jnp.where(own_head, s, NEG)
        online_update(slice(0, hps * t_new), p, s, v)

    @pl.when(kb == pl.num_programs(1) - 1)
    def _():
        for h in range(n_heads):
            p, r = h % SUBLANES, h // SUBLANES
            cols = slice(h * hd, (h + 1) * hd)
            s = _dot_nt(q_ref[:, cols], kn_ref[:, cols]) + (cq_ref[:, h:h + 1] - ckn_ref[h:h + 1, :])
            online_update(slice(r * t_new, (r + 1) * t_new), p, _causal(s), vn_ref[:, cols])
        for p in range(SUBLANES):
            cols = slice(p * hd, (p + 1) * hd)
            acc_s[:, cols] = acc_s[:, cols] / l_s[p]
        att = jnp.concatenate([acc_s[r * t_new:(r + 1) * t_new, :] for r in range(hps)], axis=1)
        o_ref[...] = (att * _rms_scale(att) * g_ref[...]).astype(o_ref.dtype)


def decode_attention(q, k_new, v_new, row0, cache_k, cache_v, layer, cum_new, cum_cache_s, cum_new_t, g_att,
                     n_batch, t_new, n_heads, tk=512):
    width = n_heads * HEAD_DIM
    past = cache_k.shape[2] // n_heads
    hps = n_heads // SUBLANES
    assert n_heads % SUBLANES == 0 and hps & (hps - 1) == 0
    tk = _tile(past, tk, LANES)
    r0 = row0 // t_new
    kernel = functools.partial(_decode_kernel, n_heads=n_heads, t_new=t_new, tk=tk)
    new_spec = pl.BlockSpec((t_new, width), lambda b, kb: (r0 + b, 0))
    cache_spec = pl.BlockSpec((None, None, tk * n_heads, HEAD_DIM), lambda b, kb: (layer, b, kb, 0))
    return pl.pallas_call(
        kernel,
        grid=(n_batch, past // tk),
        in_specs=[new_spec, new_spec, new_spec, cache_spec, cache_spec,
                  pl.BlockSpec((t_new, n_heads), lambda b, kb: (b, 0)),
                  pl.BlockSpec((None, None, SUBLANES, tk * hps), lambda b, kb: (layer, b, 0, kb)),
                  pl.BlockSpec((None, n_heads, t_new), lambda b, kb: (b, 0, 0)),
                  pl.BlockSpec((1, width), lambda b, kb: (0, 0))],
        out_specs=pl.BlockSpec((t_new, width), lambda b, kb: (b, 0)),
        out_shape=jax.ShapeDtypeStruct((n_batch * t_new, width), BF16),
        scratch_shapes=[pltpu.VMEM((SUBLANES, hps * t_new, 1), F32),
                        pltpu.VMEM((SUBLANES, hps * t_new, 1), F32),
                        pltpu.VMEM((hps * t_new, SUBLANES * HEAD_DIM), F32)],
        compiler_params=_params("parallel", "arbitrary"),
        name="decode_attention",
    )(q, k_new, v_new, cache_k, cache_v, cum_new, cum_cache_s, cum_new_t, g_att)


def _conv_kernel(a_ref, g_ref, pre_ref, w_ref, b_ref, lng_ref, lnb_ref, gc_ref, o_ref, st_ref,
                 u_s, y_s, *, tt, rw, cw):
    t = pl.program_id(1)
    width = u_s.shape[1]

    @pl.when(t == 0)
    def _():
        u_s[0:HALO, :] = pre_ref[...]

    @pl.when(t > 0)
    def _():
        u_s[0:HALO, :] = u_s[tt:tt + HALO, :]

    a = a_ref[...]
    u_s[HALO:HALO + tt, :] = a * jax.nn.sigmoid(g_ref[...])

    first = HALO - (CONV_WIDTH - 1)
    for r0 in range(0, tt, rw):
        for c in range(width // cw):
            cols = slice(c * cw, (c + 1) * cw)
            acc = jnp.broadcast_to(b_ref[:, cols], (rw, cw))
            for j in range(CONV_WIDTH):
                acc = acc + u_s[r0 + first + j:r0 + first + j + rw, cols] * w_ref[j:j + 1, cols]
            y_s[r0:r0 + rw, cols] = acc

    y = y_s[...]
    mu = jnp.mean(y, axis=-1, keepdims=True)
    yc = y - mu
    y = yc * lax.rsqrt(jnp.mean(yc * yc, axis=-1, keepdims=True) + EPS) * lng_ref[...] + lnb_ref[...]
    y = y * jax.nn.sigmoid(y)
    o_ref[...] = (y * _rms_scale(y) * gc_ref[...]).astype(o_ref.dtype)

    @pl.when(t == pl.num_programs(1) - 1)
    def _():
        st_ref[...] = u_s[tt + first:tt + HALO, :]


def conv_branch(glu, row0, prefix, w_dw, b_dw, cln_g, cln_b, g_conv, n_seq, seq, tt=256):
    width = w_dw.shape[1]
    tt = _tile(seq, tt, 8)
    nt = seq // tt
    r0 = row0 // tt
    rw = _tile(tt, 32, 8)
    kernel = functools.partial(_conv_kernel, tt=tt, rw=rw, cw=256)
    vec = lambda n: pl.BlockSpec((n, width), lambda s, t: (0, 0))
    return pl.pallas_call(
        kernel,
        grid=(n_seq, nt),
        in_specs=[pl.BlockSpec((tt, width), lambda s, t: (r0 + s * nt + t, 0)),
                  pl.BlockSpec((tt, width), lambda s, t: (r0 + s * nt + t, 1)),
                  pl.BlockSpec((None, HALO, width), lambda s, t: (s, 0, 0)),
                  vec(CONV_WIDTH), vec(1), vec(1), vec(1), vec(1)],
        out_specs=[pl.BlockSpec((tt, width), lambda s, t: (s * nt + t, 0)),
                   pl.BlockSpec((None, CONV_WIDTH - 1, width), lambda s, t: (s, 0, 0))],
        out_shape=[jax.ShapeDtypeStruct((n_seq * seq, width), BF16),
                   jax.ShapeDtypeStruct((n_seq, CONV_WIDTH - 1, width), F32)],
        scratch_shapes=[pltpu.VMEM((tt + HALO, width), F32),
                        pltpu.VMEM((tt, width), F32)],
        compiler_params=_params("parallel", "arbitrary"),
        name="conv_branch",
    )(glu, glu, prefix, w_dw, b_dw, cln_g, cln_b, g_conv)


def kernel(x_prompt, x_sample, cache_k, cache_v, cache_logf, state_conv, ln_mix, w_in, b_f,
           w_dw, b_dw, cln_g, cln_b, g_att, g_conv, w_out, ln_ffn, w_gate, w_up, w_down, g_final):
    n_b, seq, d = x_prompt.shape
    n_s, t_new, _ = x_sample.shape
    depth, _, past, n_heads, hd = cache_k.shape
    assert hd == HEAD_DIM
    att_w = n_heads * hd
    conv_w = w_dw.shape[2]
    assert conv_w == att_w and w_in.shape[2] == 3 * att_w + n_heads + 2 * conv_w
    mp = n_b * seq
    ms = n_s * t_new
    row = lambda v: v.reshape(1, -1)
    q_scale = hd ** -0.5 * LOG2E

    x = jnp.concatenate([x_prompt.reshape(mp, d), x_sample.reshape(ms, d)], axis=0)
    cum_cache = cumsum_time(cache_logf.reshape(depth * n_s, past, n_heads)).reshape(depth, n_s, past, n_heads)
    hps = n_heads // SUBLANES
    cum_cache_s = jnp.transpose(cum_cache.reshape(depth, n_s, past, hps, SUBLANES), (0, 1, 4, 2, 3))
    cum_cache_s = cum_cache_s.reshape(depth, n_s, SUBLANES, past * hps) * LOG2E
    ck = cache_k.reshape(depth, n_s, past * n_heads, hd)
    cv = cache_v.reshape(depth, n_s, past * n_heads, hd)
    zero_prefix = jnp.zeros((n_b, HALO, conv_w), F32)
    gate0 = 3 * att_w
    w_f = jnp.pad(w_in[:, :, gate0:gate0 + n_heads], ((0, 0), (0, 0), (0, LANES - n_heads))).astype(BF16)
    w_glu = w_in[:, :, gate0 + n_heads:]
    w_down_b = w_down.astype(BF16)

    outs = {k: [] for k in ("kp", "vp", "fp", "cp", "ks", "vs", "fs", "cs")}
    for l in range(depth):
        b_fp = jnp.pad(b_f[l], (0, LANES - n_heads)).reshape(1, LANES)

        h, logf = norm_gate(x, row(ln_mix[l]), w_f[l], b_fp, n_heads)
        q = proj(h, w_in, l, 0, att_w, BF16, out_scale=q_scale)
        k, kb = proj2(h, w_in, l, att_w, att_w)
        v, vb = proj2(h, w_in, l, 2 * att_w, att_w)
        glu = proj(h, w_glu, l, 0, 2 * conv_w, F32)

        logf_p = logf[:mp].reshape(n_b, seq, n_heads)
        logf_s = logf[mp:].reshape(n_s, t_new, n_heads)
        cum_p = cumsum_time(logf_p)
        cum_s = cumsum_time(logf_s, offset=cum_cache[l][:, -1, :])

        att_p = flash_attention(q, kb, vb, cum_p.reshape(mp, n_heads), n_b, seq, n_heads)
        att_p = rmsnorm(att_p, row(g_att[l]), BF16)
        cum_s2 = cum_s * LOG2E
        att_s = decode_attention(q, kb, vb, mp, ck, cv, l, cum_s2.reshape(ms, n_heads),
                                 cum_cache_s, jnp.swapaxes(cum_s2, 1, 2), row(g_att[l]), n_s, t_new, n_heads)

        conv_args = (w_dw[l], row(b_dw[l]), row(cln_g[l]), row(cln_b[l]), row(g_conv[l]))
        conv_p, st_p = conv_branch(glu, 0, zero_prefix, *conv_args, n_b, seq)
        prefix_s = jnp.pad(state_conv[l], ((0, 0), (HALO - (CONV_WIDTH - 1), 0), (0, 0)))
        conv_s, st_s = conv_branch(glu, mp, prefix_s, *conv_args, n_s, t_new)

        att = jnp.concatenate([att_p, att_s], axis=0)
        conv = jnp.concatenate([conv_p, conv_s], axis=0)
        x = out_proj(att, conv, w_out, l, x)

        h2 = rmsnorm(x, row(ln_ffn[l]), BF16)
        hidden = ffn_up(h2, w_gate, w_up, l)
        x = ffn_down(hidden, w_down_b, l, x)

        outs["kp"].append(k[:mp].reshape(n_b, seq, n_heads, hd))
        outs["vp"].append(v[:mp].reshape(n_b, seq, n_heads, hd))
        outs["fp"].append(logf_p)
        outs["cp"].append(st_p)
        outs["ks"].append(k[mp:].reshape(n_s, t_new, n_heads, hd))
        outs["vs"].append(v[mp:].reshape(n_s, t_new, n_heads, hd))
        outs["fs"].append(logf_s)
        outs["cs"].append(st_s)

    y_p = rmsnorm(x, row(g_final), F32, 0, mp)
    y_s = rmsnorm(x, row(g_final), F32, mp, ms)
    stk = lambda k: jnp.stack(outs[k])
    return (y_p.reshape(n_b, seq, d), y_s.reshape(n_s, t_new, d),
            stk("kp"), stk("vp"), stk("fp"), stk("cp"),
            stk("ks"), stk("vs"), stk("fs"), stk("cs"))
```

```python
import functools
import math

import jax
import jax.numpy as jnp
from jax import lax
from jax.experimental import pallas as pl
from jax.experimental.pallas import tpu as pltpu

HEAD_DIM = 128
CONV_WIDTH = 31
HALO = 32
EPS = 1e-6
NEG = -1e30
LOG2E = math.log2(math.e)
LANES = 128
SUBLANES = 8
CONV_ROW_STRIDE = 4
CONV_GROUP = SUBLANES * CONV_ROW_STRIDE
VMEM_LIMIT = 56 * 1024 * 1024

BF16 = jnp.bfloat16
F32 = jnp.float32


def _tile(dim, pref, mult):
    t = min(pref, dim)
    t -= t % mult
    while t >= mult:
        if dim % t == 0:
            return t
        t -= mult
    return dim


def _params(*sem):
    return pltpu.CompilerParams(dimension_semantics=sem, vmem_limit_bytes=VMEM_LIMIT)


def _dot(a, b):
    return jnp.dot(a, b, preferred_element_type=F32)


def _dot_nt(a, b):
    return lax.dot_general(a, b, (((1,), (1,)), ((), ())), preferred_element_type=F32)


def _log_sigmoid(x):
    return jnp.minimum(x, 0.0) - jnp.log1p(jnp.exp(-jnp.abs(x)))


def _rms_scale(x):
    return lax.rsqrt(jnp.mean(x * x, axis=-1, keepdims=True) + EPS)


def _norm_gate_kernel(x_ref, g_ref, wf_ref, bf_ref, h_ref, lf_ref):
    x = x_ref[...]
    h = (x * _rms_scale(x) * g_ref[...]).astype(BF16)
    h_ref[...] = h
    f = _dot(h, wf_ref[...].astype(BF16)) + bf_ref[...]
    lf_ref[...] = _log_sigmoid(f)[:, :lf_ref.shape[1]]


def norm_gate(x, g, w, layer, col0, bf, n_heads):
    m, d = x.shape
    tm = _tile(m, 256, 8)
    assert col0 % LANES == 0 and n_heads <= LANES
    return pl.pallas_call(
        _norm_gate_kernel,
        grid=(m // tm,),
        in_specs=[pl.BlockSpec((tm, d), lambda i: (i, 0)),
                  pl.BlockSpec((1, d), lambda i: (0, 0)),
                  pl.BlockSpec((None, d, LANES), lambda i: (layer, 0, col0 // LANES)),
                  pl.BlockSpec((1, LANES), lambda i: (0, 0))],
        out_specs=[pl.BlockSpec((tm, d), lambda i: (i, 0)),
                   pl.BlockSpec((tm, n_heads), lambda i: (i, 0))],
        out_shape=[jax.ShapeDtypeStruct((m, d), BF16),
                   jax.ShapeDtypeStruct((m, n_heads), F32)],
        compiler_params=_params("parallel"),
        name="norm_gate",
    )(x, g, w, bf)


def _rmsnorm_kernel(x_ref, g_ref, o_ref):
    x = x_ref[...]
    o_ref[...] = (x * _rms_scale(x) * g_ref[...]).astype(o_ref.dtype)


def rmsnorm(x, g, out_dtype, row0=0, rows=None, out_rows=None):
    d = x.shape[1]
    rows = x.shape[0] if rows is None else rows
    out_rows = rows if out_rows is None else out_rows
    tm = _tile(math.gcd(rows, row0) if row0 else rows, 256, 8)
    r0 = row0 // tm
    return pl.pallas_call(
        _rmsnorm_kernel,
        grid=(rows // tm,),
        in_specs=[pl.BlockSpec((tm, d), lambda i: (r0 + i, 0)),
                  pl.BlockSpec((1, d), lambda i: (0, 0))],
        out_specs=pl.BlockSpec((tm, d), lambda i: (i, 0)),
        out_shape=jax.ShapeDtypeStruct((out_rows, d), out_dtype),
        compiler_params=_params("parallel"),
        name="rmsnorm",
    )(x, g)


def _cumsum_kernel(x_ref, off_ref, o_ref, *, rows_per_seq, n_heads):
    x = x_ref[...]
    tb = x.shape[0]
    head_bits = n_heads.bit_length() - 1
    seq_bits = rows_per_seq.bit_length() - 1
    li = lax.broadcasted_iota(jnp.int32, (LANES, LANES), 0)
    lj = lax.broadcasted_iota(jnp.int32, (LANES, LANES), 1)
    same_head = (li & (n_heads - 1)) == (lj & (n_heads - 1))
    upper = (same_head & ((li >> head_bits) <= (lj >> head_bits))).astype(F32)
    every = same_head.astype(F32)
    hi = lax.Precision.HIGHEST
    in_row = jnp.dot(x, upper, precision=hi, preferred_element_type=F32)
    row_tot = jnp.dot(x, every, precision=hi, preferred_element_type=F32)
    ri = lax.broadcasted_iota(jnp.int32, (tb, tb), 0)
    rj = lax.broadcasted_iota(jnp.int32, (tb, tb), 1)
    earlier = (((ri >> seq_bits) == (rj >> seq_bits)) & (rj < ri)).astype(F32)
    before = jnp.dot(earlier, row_tot, precision=hi, preferred_element_type=F32)
    o_ref[...] = in_row + before + off_ref[...]


def cumsum_time(logf, offset=None):
    b, t, h = logf.shape
    per_row = LANES // h
    rows = t // per_row
    assert h & (h - 1) == 0 and rows & (rows - 1) == 0 and rows * per_row == t
    x = logf.reshape(b * rows, LANES)
    if offset is None:
        off = jnp.zeros((b * rows, LANES), F32)
    else:
        off = jnp.broadcast_to(offset[:, None, None, :], (b, rows, per_row, h)).reshape(b * rows, LANES)
    tb = rows * _tile(b, max(1, 512 // rows), 1)
    out = pl.pallas_call(
        functools.partial(_cumsum_kernel, rows_per_seq=rows, n_heads=h),
        grid=(b * rows // tb,),
        in_specs=[pl.BlockSpec((tb, LANES), lambda i: (i, 0)),
                  pl.BlockSpec((tb, LANES), lambda i: (i, 0))],
        out_specs=pl.BlockSpec((tb, LANES), lambda i: (i, 0)),
        out_shape=jax.ShapeDtypeStruct((b * rows, LANES), F32),
        compiler_params=_params("parallel"),
        name="cumsum_time",
    )(x, off)
    return out.reshape(b, t, h)


def _cast_on_first_row_tile(w_refs, wb_refs):
    @pl.when(pl.program_id(1) == 0)
    def _():
        for w_ref, wb_ref in zip(w_refs, wb_refs):
            wb_ref[...] = w_ref[...].astype(BF16)


def _proj_kernel(x_ref, w_ref, o_ref, wb_s, *, out_scale):
    _cast_on_first_row_tile([w_ref], [wb_s])
    acc = _dot(x_ref[...], wb_s[...])
    if out_scale != 1.0:
        acc = acc * out_scale
    o_ref[...] = acc.astype(o_ref.dtype)


def proj(x, w, layer, col0, n, out_dtype, out_scale=1.0, tm=1536, tn=512):
    m, k = x.shape
    tm, tn = _tile(m, tm, 8), _tile(math.gcd(n, col0) if col0 else n, tn, LANES)
    c0 = col0 // tn
    return pl.pallas_call(
        functools.partial(_proj_kernel, out_scale=out_scale),
        grid=(n // tn, m // tm),
        in_specs=[pl.BlockSpec((tm, k), lambda j, i: (i, 0)),
                  pl.BlockSpec((None, k, tn), lambda j, i: (layer, 0, c0 + j))],
        out_specs=pl.BlockSpec((tm, tn), lambda j, i: (i, j)),
        out_shape=jax.ShapeDtypeStruct((m, n), out_dtype),
        scratch_shapes=[pltpu.VMEM((k, tn), BF16)],
        compiler_params=_params("arbitrary", "arbitrary"),
        name="proj",
    )(x, w)


def _proj2_kernel(x_ref, w_ref, o_ref, ob_ref, wb_s):
    _cast_on_first_row_tile([w_ref], [wb_s])
    acc = _dot(x_ref[...], wb_s[...])
    o_ref[...] = acc
    ob_ref[...] = acc.astype(BF16)


def proj2(x, w, layer, col0, n, tm=1024, tn=512):
    m, k = x.shape
    tm, tn = _tile(m, tm, 8), _tile(math.gcd(n, col0) if col0 else n, tn, LANES)
    c0 = col0 // tn
    out = pl.BlockSpec((tm, tn), lambda j, i: (i, j))
    return pl.pallas_call(
        _proj2_kernel,
        grid=(n // tn, m // tm),
        in_specs=[pl.BlockSpec((tm, k), lambda j, i: (i, 0)),
                  pl.BlockSpec((None, k, tn), lambda j, i: (layer, 0, c0 + j))],
        out_specs=[out, out],
        out_shape=[jax.ShapeDtypeStruct((m, n), F32), jax.ShapeDtypeStruct((m, n), BF16)],
        scratch_shapes=[pltpu.VMEM((k, tn), BF16)],
        compiler_params=_params("arbitrary", "arbitrary"),
        name="proj2",
    )(x, w)


def _out_proj_kernel(a_ref, c_ref, wa_ref, wc_ref, x_ref, o_ref, wab_s, wcb_s):
    _cast_on_first_row_tile([wa_ref, wc_ref], [wab_s, wcb_s])
    o_ref[...] = x_ref[...] + _dot(a_ref[...], wab_s[...]) + _dot(c_ref[...], wcb_s[...])


def out_proj(att, conv, w_out, layer, x, tm=1024, tn=512):
    m, ka = att.shape
    kc = conv.shape[1]
    n = w_out.shape[2]
    assert ka == kc
    tm, tn = _tile(m, tm, 8), _tile(n, tn, LANES)
    return pl.pallas_call(
        _out_proj_kernel,
        grid=(n // tn, m // tm),
        in_specs=[pl.BlockSpec((tm, ka), lambda j, i: (i, 0)),
                  pl.BlockSpec((tm, kc), lambda j, i: (i, 0)),
                  pl.BlockSpec((None, ka, tn), lambda j, i: (layer, 0, j)),
                  pl.BlockSpec((None, kc, tn), lambda j, i: (layer, 1, j)),
                  pl.BlockSpec((tm, tn), lambda j, i: (i, j))],
        out_specs=pl.BlockSpec((tm, tn), lambda j, i: (i, j)),
        out_shape=jax.ShapeDtypeStruct((m, n), F32),
        scratch_shapes=[pltpu.VMEM((ka, tn), BF16), pltpu.VMEM((kc, tn), BF16)],
        compiler_params=_params("arbitrary", "arbitrary"),
        name="out_proj",
    )(att, conv, w_out, w_out, x)


def _ffn_up_kernel(h_ref, wg_ref, wu_ref, o_ref, wgb_s, wub_s):
    _cast_on_first_row_tile([wg_ref, wu_ref], [wgb_s, wub_s])
    h = h_ref[...]
    g = _dot(h, wgb_s[...])
    u = _dot(h, wub_s[...])
    o_ref[...] = (g * jax.nn.sigmoid(g) * u).astype(o_ref.dtype)


def ffn_up(h, w_gate, w_up, layer, tm=1536, tn=256):
    m, k = h.shape
    n = w_gate.shape[2]
    tm, tn = _tile(m, tm, 8), _tile(n, tn, LANES)
    wspec = pl.BlockSpec((None, k, tn), lambda j, i: (layer, 0, j))
    return pl.pallas_call(
        _ffn_up_kernel,
        grid=(n // tn, m // tm),
        in_specs=[pl.BlockSpec((tm, k), lambda j, i: (i, 0)), wspec, wspec],
        out_specs=pl.BlockSpec((tm, tn), lambda j, i: (i, j)),
        out_shape=jax.ShapeDtypeStruct((m, n), BF16),
        scratch_shapes=[pltpu.VMEM((k, tn), BF16), pltpu.VMEM((k, tn), BF16)],
        compiler_params=_params("arbitrary", "arbitrary"),
        name="ffn_up",
    )(h, w_gate, w_up)


def _ffn_down_kernel(h_ref, w_ref, x_ref, o_ref):
    o_ref[...] = x_ref[...] + _dot(h_ref[...], w_ref[...])


def ffn_down(hidden, wd, layer, x, tm=512, tn=512):
    m, k = hidden.shape
    n = wd.shape[2]
    tm, tn = _tile(m, tm, 8), _tile(n, tn, LANES)
    return pl.pallas_call(
        _ffn_down_kernel,
        grid=(m // tm, n // tn),
        in_specs=[pl.BlockSpec((tm, k), lambda i, j: (i, 0)),
                  pl.BlockSpec((None, k, tn), lambda i, j: (layer, 0, j)),
                  pl.BlockSpec((tm, tn), lambda i, j: (i, j))],
        out_specs=pl.BlockSpec((tm, tn), lambda i, j: (i, j)),
        out_shape=jax.ShapeDtypeStruct((m, n), F32),
        compiler_params=_params("parallel", "parallel"),
        name="ffn_down",
    )(hidden, wd, x)


def _causal(s):
    row = lax.broadcasted_iota(jnp.int32, s.shape, 0)
    col = lax.broadcasted_iota(jnp.int32, s.shape, 1)
    return jnp.where(col <= row, s, NEG)


def _column(x, h):
    lane = lax.broadcasted_iota(jnp.int32, x.shape, 1)
    return jnp.sum(jnp.where(lane == h, x, 0.0), axis=-1, keepdims=True)


def _split3(x):
    hi = x.astype(BF16).astype(F32)
    r = x - hi
    mid = r.astype(BF16).astype(F32)
    lo = (r - mid).astype(BF16).astype(F32)
    return hi, mid, lo


def _bias_lanes(d, query_side):
    hi, mid, lo = _split3(d)
    lane = lax.broadcasted_iota(jnp.int32, (d.shape[0], LANES), 1)
    one = jnp.where(lane < 6, 1.0, 0.0)
    if query_side:
        v = jnp.where(lane == 0, hi, jnp.where(lane == 1, mid, jnp.where(lane == 2, lo, one)))
    else:
        v = jnp.where(lane == 3, -hi, jnp.where(lane == 4, -mid, jnp.where(lane == 5, -lo, one)))
    return v.astype(BF16)


def _flash_kernel(q_ref, k_ref, v_ref, cq_ref, ck_ref, o_ref, qa_s, ka_s, va_s, m_s, acc_s, *, tq, halves):
    h = pl.program_id(1)
    qi = pl.program_id(2)
    hd = HEAD_DIM

    @pl.when(qi == 0)
    def _():
        ka_s[:, :hd] = k_ref[...]
        ka_s[:, hd:] = _bias_lanes(_column(ck_ref[...], h) * LOG2E, False)
        va_s[:, :hd] = v_ref[...]
        lane = lax.broadcasted_iota(jnp.int32, (va_s.shape[0], LANES), 1)
        va_s[:, hd:] = jnp.where(lane == 0, 1.0, 0.0).astype(BF16)

    qa_s[:, :hd] = q_ref[...]
    qa_s[:, hd:] = _bias_lanes(_column(cq_ref[...], h) * LOG2E, True)
    m_s[...] = jnp.full(m_s.shape, NEG, F32)
    acc_s[...] = jnp.zeros(acc_s.shape, F32)
    th = tq // halves

    def scores(ki):
        k0 = pl.multiple_of(ki * tq, tq)
        ka = ka_s[pl.ds(k0, tq), :]
        return [_dot_nt(qa_s[r * th:(r + 1) * th, :], ka) for r in range(halves)]

    def update(ki, s_halves, masked):
        k0 = pl.multiple_of(ki * tq, tq)
        va = va_s[pl.ds(k0, tq), :]
        for r, s in enumerate(s_halves):
            rows = slice(r * th, (r + 1) * th)
            if masked:
                row = lax.broadcasted_iota(jnp.int32, s.shape, 0) + r * th
                col = lax.broadcasted_iota(jnp.int32, s.shape, 1)
                s = jnp.where(col <= row, s, NEG)
            m_prev = m_s[rows, :]
            m_new = jnp.maximum(m_prev, jnp.max(s, axis=-1, keepdims=True))
            p = jnp.exp2(s - m_new).astype(BF16)
            acc_s[rows, :] = jnp.exp2(m_prev - m_new) * acc_s[rows, :] + _dot(p, va)
            m_s[rows, :] = m_new

    def body(ki, s_halves):
        s_next = scores(ki + 1)
        update(ki, s_halves, False)
        return s_next

    s_last = lax.fori_loop(0, qi, body, scores(0))
    update(qi, s_last, True)
    acc = acc_s[...]
    o_ref[...] = acc[:, :hd] / acc[:, hd:hd + 1]


def flash_attention(q, k, v, cum, n_batch, seq, n_heads, tq=512):
    tq = _tile(seq, tq, LANES)
    nq = seq // tq
    hd = HEAD_DIM
    kernel = functools.partial(_flash_kernel, tq=tq, halves=2)
    return pl.pallas_call(
        kernel,
        grid=(n_batch, n_heads, nq),
        in_specs=[pl.BlockSpec((tq, hd), lambda b, h, qi: (b * nq + qi, h)),
                  pl.BlockSpec((seq, hd), lambda b, h, qi: (b, h)),
                  pl.BlockSpec((seq, hd), lambda b, h, qi: (b, h)),
                  pl.BlockSpec((tq, n_heads), lambda b, h, qi: (b * nq + qi, 0)),
                  pl.BlockSpec((seq, n_heads), lambda b, h, qi: (b, 0))],
        out_specs=pl.BlockSpec((tq, hd), lambda b, h, qi: (b * nq + qi, h)),
        out_shape=jax.ShapeDtypeStruct((n_batch * seq, n_heads * hd), F32),
        scratch_shapes=[pltpu.VMEM((tq, 2 * hd), BF16),
                        pltpu.VMEM((seq, 2 * hd), BF16),
                        pltpu.VMEM((seq, 2 * hd), BF16),
                        pltpu.VMEM((tq, 1), F32),
                        pltpu.VMEM((tq, 2 * hd), F32)],
        compiler_params=_params("parallel", "parallel", "arbitrary"),
        name="flash_attention",
    )(q, k, v, cum, cum)


def _decode_kernel(q_ref, kn_ref, vn_ref, kc_ref, vc_ref, cq_ref, ckc_ref, ckn_ref, g_ref, into_ref, o_ref,
                   m_s, l_s, acc_s, *, n_heads, t_new, tk):
    del into_ref
    kb = pl.program_id(1)
    hd = HEAD_DIM
    hps = n_heads // SUBLANES
    n_rows = tk * hps

    @pl.when(kb == 0)
    def _():
        m_s[...] = jnp.full(m_s.shape, NEG, F32)
        l_s[...] = jnp.zeros(l_s.shape, F32)
        acc_s[...] = jnp.zeros(acc_s.shape, F32)

    def online_update(rows, p, s, v):
        cols = slice(p * hd, (p + 1) * hd)
        m_prev = m_s[p, rows, :]
        m_new = jnp.maximum(m_prev, jnp.max(s, axis=-1, keepdims=True))
        alpha = jnp.exp2(m_prev - m_new)
        e = jnp.exp2(s - m_new)
        l_s[p, rows, :] = alpha * l_s[p, rows, :] + jnp.sum(e, axis=-1, keepdims=True)
        acc_s[rows, cols] = alpha * acc_s[rows, cols] + _dot(e.astype(BF16), v)
        m_s[p, rows, :] = m_new

    if hps > 1:
        shape = (hps * t_new, n_rows)
        row = lax.broadcasted_iota(jnp.int32, shape, 0)
        col = lax.broadcasted_iota(jnp.int32, shape, 1)
        row_head = sum((row >= r * t_new).astype(jnp.int32) for r in range(1, hps))
        own_head = row_head == (col & (hps - 1))

    heads_of = lambda p: [p + SUBLANES * r for r in range(hps)]
    for p in range(SUBLANES):
        q = jnp.concatenate([q_ref[:, h * hd:(h + 1) * hd] for h in heads_of(p)], axis=0)
        dq = jnp.concatenate([cq_ref[:, h:h + 1] for h in heads_of(p)], axis=0)
        k = kc_ref[pl.ds(p, n_rows, stride=SUBLANES), :].astype(BF16)
        v = vc_ref[pl.ds(p, n_rows, stride=SUBLANES), :].astype(BF16)
        s = _dot_nt(q, k) + (dq - ckc_ref[p:p + 1, :])
        if hps > 1:
            s = jnp.where(own_head, s, NEG)
        online_update(slice(0, hps * t_new), p, s, v)

    @pl.when(kb == pl.num_programs(1) - 1)
    def _():
        for h in range(n_heads):
            p, r = h % SUBLANES, h // SUBLANES
            cols = slice(h * hd, (h + 1) * hd)
            s = _dot_nt(q_ref[:, cols], kn_ref[:, cols]) + (cq_ref[:, h:h + 1] - ckn_ref[h:h + 1, :])
            online_update(slice(r * t_new, (r + 1) * t_new), p, _causal(s), vn_ref[:, cols])
        for p in range(SUBLANES):
            cols = slice(p * hd, (p + 1) * hd)
            acc_s[:, cols] = acc_s[:, cols] / l_s[p]
        att = jnp.concatenate([acc_s[r * t_new:(r + 1) * t_new, :] for r in range(hps)], axis=1)
        o_ref[...] = (att * _rms_scale(att) * g_ref[...]).astype(o_ref.dtype)


def decode_attention(q, k_new, v_new, row0, cache_k, cache_v, layer, cum_new, cum_cache_s, cum_new_t, g_att,
                     into, n_batch, t_new, n_heads, tk=512):
    width = n_heads * HEAD_DIM
    past = cache_k.shape[2] // n_heads
    hps = n_heads // SUBLANES
    assert n_heads % SUBLANES == 0 and hps & (hps - 1) == 0
    tk = _tile(past, tk, LANES)
    r0 = row0 // t_new
    kernel = functools.partial(_decode_kernel, n_heads=n_heads, t_new=t_new, tk=tk)
    new_spec = pl.BlockSpec((t_new, width), lambda b, kb: (r0 + b, 0))
    cache_spec = pl.BlockSpec((None, None, tk * n_heads, HEAD_DIM), lambda b, kb: (layer, b, kb, 0))
    return pl.pallas_call(
        kernel,
        grid=(n_batch, past // tk),
        in_specs=[new_spec, new_spec, new_spec, cache_spec, cache_spec,
                  pl.BlockSpec((t_new, n_heads), lambda b, kb: (b, 0)),
                  pl.BlockSpec((None, None, SUBLANES, tk * hps), lambda b, kb: (layer, b, 0, kb)),
                  pl.BlockSpec((None, n_heads, t_new), lambda b, kb: (b, 0, 0)),
                  pl.BlockSpec((1, width), lambda b, kb: (0, 0)),
                  pl.BlockSpec(memory_space=pl.ANY)],
        out_specs=pl.BlockSpec((t_new, width), lambda b, kb: (r0 + b, 0)),
        out_shape=jax.ShapeDtypeStruct(into.shape, into.dtype),
        input_output_aliases={9: 0},
        scratch_shapes=[pltpu.VMEM((SUBLANES, hps * t_new, 1), F32),
                        pltpu.VMEM((SUBLANES, hps * t_new, 1), F32),
                        pltpu.VMEM((hps * t_new, SUBLANES * HEAD_DIM), F32)],
        compiler_params=_params("parallel", "arbitrary"),
        name="decode_attention",
    )(q, k_new, v_new, cache_k, cache_v, cum_new, cum_cache_s, cum_new_t, g_att, into)


def _conv_kernel(*refs, tt):
    a_ref, g_ref, pre_ref, w_ref, b_ref, lng_ref, lnb_ref, gc_ref = refs[:8]
    o_ref, st_ref, u_s, y_s = refs[-4:]
    t = pl.program_id(1)
    n_slabs = u_s.shape[0]
    slab = lambda c: slice(c * LANES, (c + 1) * LANES)

    @pl.when(t == 0)
    def _():
        for c in range(n_slabs):
            u_s[c, 0:HALO, :] = pre_ref[:, slab(c)]

    @pl.when(t > 0)
    def _():
        for c in range(n_slabs):
            u_s[c, 0:HALO, :] = u_s[c, tt:tt + HALO, :]

    u = a_ref[...] * jax.nn.sigmoid(g_ref[...])
    for c in range(n_slabs):
        u_s[c, HALO:HALO + tt, :] = u[:, slab(c)]

    first = HALO - (CONV_WIDTH - 1)

    def group(gi, carry):
        r0 = pl.multiple_of(gi * CONV_GROUP, CONV_GROUP)
        for c in range(n_slabs):
            taps = [w_ref[j:j + 1, slab(c)] for j in range(CONV_WIDTH)]
            win = [u_s[c, pl.ds(r0 + first + k, SUBLANES, stride=CONV_ROW_STRIDE), :]
                   for k in range(CONV_WIDTH + CONV_ROW_STRIDE - 1)]
            for ph in range(CONV_ROW_STRIDE):
                acc = jnp.broadcast_to(b_ref[:, slab(c)], (SUBLANES, LANES))
                for j in range(CONV_WIDTH):
                    acc = acc + win[j + ph] * taps[j]
                y_s[c, pl.ds(r0 + ph, SUBLANES, stride=CONV_ROW_STRIDE), :] = acc
        return carry

    lax.fori_loop(0, tt // CONV_GROUP, group, 0)

    y = jnp.concatenate([y_s[c] for c in range(n_slabs)], axis=1)
    mu = jnp.mean(y, axis=-1, keepdims=True)
    yc = y - mu
    y = yc * lax.rsqrt(jnp.mean(yc * yc, axis=-1, keepdims=True) + EPS) * lng_ref[...] + lnb_ref[...]
    y = y * jax.nn.sigmoid(y)
    o_ref[...] = (y * _rms_scale(y) * gc_ref[...]).astype(o_ref.dtype)

    @pl.when(t == pl.num_programs(1) - 1)
    def _():
        st_ref[...] = jnp.concatenate([u_s[c, tt + first:tt + HALO, :] for c in range(n_slabs)], axis=1)


def conv_branch(glu, row0, prefix, w_dw, b_dw, cln_g, cln_b, g_conv, n_seq, seq, out_rows, into=None, tt=256):
    width = w_dw.shape[1]
    tt = _tile(math.gcd(seq, row0) if row0 else seq, tt, CONV_GROUP)
    assert tt % CONV_GROUP == 0 and width % LANES == 0
    nt = seq // tt
    r0 = row0 // tt
    kernel = functools.partial(_conv_kernel, tt=tt)
    vec = lambda n: pl.BlockSpec((n, width), lambda s, t: (0, 0))
    alias = {} if into is None else dict(input_output_aliases={8: 0})
    extra_specs = [] if into is None else [pl.BlockSpec(memory_space=pl.ANY)]
    extra_args = [] if into is None else [into]
    return pl.pallas_call(
        kernel,
        grid=(n_seq, nt),
        in_specs=[pl.BlockSpec((tt, width), lambda s, t: (r0 + s * nt + t, 0)),
                  pl.BlockSpec((tt, width), lambda s, t: (r0 + s * nt + t, 1)),
                  pl.BlockSpec((None, HALO, width), lambda s, t: (s, 0, 0)),
                  vec(CONV_WIDTH), vec(1), vec(1), vec(1), vec(1), *extra_specs],
        out_specs=[pl.BlockSpec((tt, width), lambda s, t: (r0 + s * nt + t, 0)),
                   pl.BlockSpec((None, CONV_WIDTH - 1, width), lambda s, t: (s, 0, 0))],
        out_shape=[jax.ShapeDtypeStruct((out_rows, width), BF16),
                   jax.ShapeDtypeStruct((n_seq, CONV_WIDTH - 1, width), F32)],
        scratch_shapes=[pltpu.VMEM((width // LANES, tt + HALO, LANES), F32),
                        pltpu.VMEM((width // LANES, tt, LANES), F32)],
        compiler_params=_params("parallel", "arbitrary"),
        name="conv_branch",
        **alias,
    )(glu, glu, prefix, w_dw, b_dw, cln_g, cln_b, g_conv, *extra_args)


def kernel(x_prompt, x_sample, cache_k, cache_v, cache_logf, state_conv, ln_mix, w_in, b_f,
           w_dw, b_dw, cln_g, cln_b, g_att, g_conv, w_out, ln_ffn, w_gate, w_up, w_down, g_final):
    n_b, seq, d = x_prompt.shape
    n_s, t_new, _ = x_sample.shape
    depth, _, past, n_heads, hd = cache_k.shape
    assert hd == HEAD_DIM
    att_w = n_heads * hd
    conv_w = w_dw.shape[2]
    assert conv_w == att_w and w_in.shape[2] == 3 * att_w + n_heads + 2 * conv_w
    mp = n_b * seq
    ms = n_s * t_new
    row = lambda v: v.reshape(1, -1)
    q_scale = hd ** -0.5 * LOG2E

    x = jnp.concatenate([x_prompt.reshape(mp, d), x_sample.reshape(ms, d)], axis=0)
    cum_cache = cumsum_time(cache_logf.reshape(depth * n_s, past, n_heads)).reshape(depth, n_s, past, n_heads)
    hps = n_heads // SUBLANES
    cum_cache_s = jnp.swapaxes(cum_cache.reshape(depth, n_s, past * hps, SUBLANES), 2, 3) * LOG2E
    ck = cache_k.reshape(depth, n_s, past * n_heads, hd)
    cv = cache_v.reshape(depth, n_s, past * n_heads, hd)
    zero_prefix = jnp.zeros((n_b, HALO, conv_w), F32)
    gate0 = 3 * att_w
    w_glu = w_in[:, :, gate0 + n_heads:]
    w_down_b = w_down.astype(BF16)
    b_fp = jnp.pad(b_f, ((0, 0), (0, LANES - n_heads)))

    outs = {k: [] for k in ("kp", "vp", "fp", "cp", "ks", "vs", "fs", "cs")}
    for l in range(depth):
        h, logf = norm_gate(x, row(ln_mix[l]), w_in, l, gate0, row(b_fp[l]), n_heads)
        q = proj(h, w_in, l, 0, att_w, BF16, out_scale=q_scale)
        k, kb = proj2(h, w_in, l, att_w, att_w)
        v, vb = proj2(h, w_in, l, 2 * att_w, att_w)
        glu = proj(h, w_glu, l, 0, 2 * conv_w, F32)

        logf_p = logf[:mp].reshape(n_b, seq, n_heads)
        logf_s = logf[mp:].reshape(n_s, t_new, n_heads)
        cum_p = cumsum_time(logf_p)
        cum_s = cumsum_time(logf_s, offset=cum_cache[l][:, -1, :])

        att_p = flash_attention(q, kb, vb, cum_p.reshape(mp, n_heads), n_b, seq, n_heads)
        att = rmsnorm(att_p, row(g_att[l]), BF16, out_rows=mp + ms)
        cum_s2 = cum_s * LOG2E
        att = decode_attention(q, kb, vb, mp, ck, cv, l, cum_s2.reshape(ms, n_heads), cum_cache_s,
                               jnp.swapaxes(cum_s2, 1, 2), row(g_att[l]), att, n_s, t_new, n_heads)

        conv_args = (w_dw[l], row(b_dw[l]), row(cln_g[l]), row(cln_b[l]), row(g_conv[l]))
        conv, st_p = conv_branch(glu, 0, zero_prefix, *conv_args, n_b, seq, mp + ms)
        prefix_s = jnp.pad(state_conv[l], ((0, 0), (HALO - (CONV_WIDTH - 1), 0), (0, 0)))
        conv, st_s = conv_branch(glu, mp, prefix_s, *conv_args, n_s, t_new, mp + ms, into=conv)

        x = out_proj(att, conv, w_out, l, x)

        h2 = rmsnorm(x, row(ln_ffn[l]), BF16)
        hidden = ffn_up(h2, w_gate, w_up, l)
        x = ffn_down(hidden, w_down_b, l, x)

        outs["kp"].append(k[:mp].reshape(n_b, seq, n_heads, hd))
        outs["vp"].append(v[:mp].reshape(n_b, seq, n_heads, hd))
        outs["fp"].append(logf_p)
        outs["cp"].append(st_p)
        outs["ks"].append(k[mp:].reshape(n_s, t_new, n_heads, hd))
        outs["vs"].append(v[mp:].reshape(n_s, t_new, n_heads, hd))
        outs["fs"].append(logf_s)
        outs["cs"].append(st_s)

    y_p = rmsnorm(x, row(g_final), F32, 0, mp)
    y_s = rmsnorm(x, row(g_final), F32, mp, ms)
    stk = lambda k: jnp.stack(outs[k])
    return (y_p.reshape(n_b, seq, d), y_s.reshape(n_s, t_new, d),
            stk("kp"), stk("vp"), stk("fp"), stk("cp"),
            stk("ks"), stk("vs"), stk("fs"), stk("cs"))
```

```python
import functools
import math

import jax
import jax.numpy as jnp
from jax import lax
from jax.experimental import pallas as pl
from jax.experimental.pallas import tpu as pltpu

HEAD_DIM = 128
CONV_WIDTH = 31
HALO = 32
EPS = 1e-6
NEG = -1e30
LOG2E = math.log2(math.e)
LANES = 128
SUBLANES = 8
CONV_ROW_STRIDE = 4
CONV_GROUP = SUBLANES * CONV_ROW_STRIDE
VMEM_LIMIT = 56 * 1024 * 1024

BF16 = jnp.bfloat16
F32 = jnp.float32


def _tile(dim, pref, mult):
    t = min(pref, dim)
    t -= t % mult
    while t >= mult:
        if dim % t == 0:
            return t
        t -= mult
    return dim


def _params(*sem):
    return pltpu.CompilerParams(dimension_semantics=sem, vmem_limit_bytes=VMEM_LIMIT)


def _dot(a, b):
    return jnp.dot(a, b, preferred_element_type=F32)


def _dot_nt(a, b):
    return lax.dot_general(a, b, (((1,), (1,)), ((), ())), preferred_element_type=F32)


def _log_sigmoid(x):
    return jnp.minimum(x, 0.0) - jnp.log1p(jnp.exp(-jnp.abs(x)))


def _rms_scale(x):
    return lax.rsqrt(jnp.mean(x * x, axis=-1, keepdims=True) + EPS)


def _norm_gate_kernel(x_ref, g_ref, wf_ref, bf_ref, h_ref, lf_ref):
    x = x_ref[...]
    h = (x * _rms_scale(x) * g_ref[...]).astype(BF16)
    h_ref[...] = h
    lf_ref[...] = _log_sigmoid(_dot_nt(h, wf_ref[...].astype(BF16)) + bf_ref[...])


def norm_gate(x, g, wt, layer, row0, bf, n_heads):
    m, d = x.shape
    tm = _tile(m, 256, 8)
    assert row0 % n_heads == 0 and n_heads % SUBLANES == 0
    return pl.pallas_call(
        _norm_gate_kernel,
        grid=(m // tm,),
        in_specs=[pl.BlockSpec((tm, d), lambda i: (i, 0)),
                  pl.BlockSpec((1, d), lambda i: (0, 0)),
                  pl.BlockSpec((None, n_heads, d), lambda i: (layer, row0 // n_heads, 0)),
                  pl.BlockSpec((1, n_heads), lambda i: (0, 0))],
        out_specs=[pl.BlockSpec((tm, d), lambda i: (i, 0)),
                   pl.BlockSpec((tm, n_heads), lambda i: (i, 0))],
        out_shape=[jax.ShapeDtypeStruct((m, d), BF16),
                   jax.ShapeDtypeStruct((m, n_heads), F32)],
        compiler_params=_params("parallel"),
        name="norm_gate",
    )(x, g, wt, bf)


def _rmsnorm_kernel(x_ref, g_ref, o_ref):
    x = x_ref[...]
    o_ref[...] = (x * _rms_scale(x) * g_ref[...]).astype(o_ref.dtype)


def rmsnorm(x, g, out_dtype, row0=0, rows=None, out_rows=None):
    d = x.shape[1]
    rows = x.shape[0] if rows is None else rows
    out_rows = rows if out_rows is None else out_rows
    tm = _tile(math.gcd(rows, row0) if row0 else rows, 256, 8)
    r0 = row0 // tm
    return pl.pallas_call(
        _rmsnorm_kernel,
        grid=(rows // tm,),
        in_specs=[pl.BlockSpec((tm, d), lambda i: (r0 + i, 0)),
                  pl.BlockSpec((1, d), lambda i: (0, 0))],
        out_specs=pl.BlockSpec((tm, d), lambda i: (i, 0)),
        out_shape=jax.ShapeDtypeStruct((out_rows, d), out_dtype),
        compiler_params=_params("parallel"),
        name="rmsnorm",
    )(x, g)


def _cumsum_kernel(x_ref, off_ref, o_ref, *, rows_per_seq, n_heads):
    x = x_ref[...]
    tb = x.shape[0]
    head_bits = n_heads.bit_length() - 1
    seq_bits = rows_per_seq.bit_length() - 1
    li = lax.broadcasted_iota(jnp.int32, (LANES, LANES), 0)
    lj = lax.broadcasted_iota(jnp.int32, (LANES, LANES), 1)
    same_head = (li & (n_heads - 1)) == (lj & (n_heads - 1))
    upper = (same_head & ((li >> head_bits) <= (lj >> head_bits))).astype(F32)
    every = same_head.astype(F32)
    hi = lax.Precision.HIGHEST
    in_row = jnp.dot(x, upper, precision=hi, preferred_element_type=F32)
    row_tot = jnp.dot(x, every, precision=hi, preferred_element_type=F32)
    ri = lax.broadcasted_iota(jnp.int32, (tb, tb), 0)
    rj = lax.broadcasted_iota(jnp.int32, (tb, tb), 1)
    earlier = (((ri >> seq_bits) == (rj >> seq_bits)) & (rj < ri)).astype(F32)
    before = jnp.dot(earlier, row_tot, precision=hi, preferred_element_type=F32)
    o_ref[...] = in_row + before + off_ref[...]


def cumsum_time(logf, offset=None):
    b, t, h = logf.shape
    per_row = LANES // h
    rows = t // per_row
    assert h & (h - 1) == 0 and rows & (rows - 1) == 0 and rows * per_row == t
    x = logf.reshape(b * rows, LANES)
    if offset is None:
        off = jnp.zeros((b * rows, LANES), F32)
    else:
        off = jnp.broadcast_to(offset[:, None, None, :], (b, rows, per_row, h)).reshape(b * rows, LANES)
    tb = rows * _tile(b, max(1, 512 // rows), 1)
    out = pl.pallas_call(
        functools.partial(_cumsum_kernel, rows_per_seq=rows, n_heads=h),
        grid=(b * rows // tb,),
        in_specs=[pl.BlockSpec((tb, LANES), lambda i: (i, 0)),
                  pl.BlockSpec((tb, LANES), lambda i: (i, 0))],
        out_specs=pl.BlockSpec((tb, LANES), lambda i: (i, 0)),
        out_shape=jax.ShapeDtypeStruct((b * rows, LANES), F32),
        compiler_params=_params("parallel"),
        name="cumsum_time",
    )(x, off)
    return out.reshape(b, t, h)


def _cast_on_first_row_tile(w_refs, wb_refs):
    @pl.when(pl.program_id(1) == 0)
    def _():
        for w_ref, wb_ref in zip(w_refs, wb_refs):
            wb_ref[...] = w_ref[...].astype(BF16)


def _proj_kernel(x_ref, w_ref, o_ref, wb_s, *, out_scale):
    _cast_on_first_row_tile([w_ref], [wb_s])
    acc = _dot_nt(x_ref[...], wb_s[...])
    if out_scale != 1.0:
        acc = acc * out_scale
    o_ref[...] = acc.astype(o_ref.dtype)


def proj(x, wt, layer, row0, n, out_dtype, out_scale=1.0, tm=1536, tn=512):
    m, k = x.shape
    tm, tn = _tile(m, tm, 8), _tile(math.gcd(n, row0) if row0 else n, tn, LANES)
    r0 = row0 // tn
    return pl.pallas_call(
        functools.partial(_proj_kernel, out_scale=out_scale),
        grid=(n // tn, m // tm),
        in_specs=[pl.BlockSpec((tm, k), lambda j, i: (i, 0)),
                  pl.BlockSpec((None, tn, k), lambda j, i: (layer, r0 + j, 0))],
        out_specs=pl.BlockSpec((tm, tn), lambda j, i: (i, j)),
        out_shape=jax.ShapeDtypeStruct((m, n), out_dtype),
        scratch_shapes=[pltpu.VMEM((tn, k), BF16)],
        compiler_params=_params("arbitrary", "arbitrary"),
        name="proj",
    )(x, wt)


def _proj2_kernel(x_ref, w_ref, o_ref, ob_ref, wb_s):
    _cast_on_first_row_tile([w_ref], [wb_s])
    acc = _dot_nt(x_ref[...], wb_s[...])
    o_ref[...] = acc
    ob_ref[...] = acc.astype(BF16)


def proj2(x, wt, layer, row0, n, tm=1024, tn=512):
    m, k = x.shape
    tm, tn = _tile(m, tm, 8), _tile(math.gcd(n, row0) if row0 else n, tn, LANES)
    r0 = row0 // tn
    out = pl.BlockSpec((tm, tn), lambda j, i: (i, j))
    return pl.pallas_call(
        _proj2_kernel,
        grid=(n // tn, m // tm),
        in_specs=[pl.BlockSpec((tm, k), lambda j, i: (i, 0)),
                  pl.BlockSpec((None, tn, k), lambda j, i: (layer, r0 + j, 0))],
        out_specs=[out, out],
        out_shape=[jax.ShapeDtypeStruct((m, n), F32), jax.ShapeDtypeStruct((m, n), BF16)],
        scratch_shapes=[pltpu.VMEM((tn, k), BF16)],
        compiler_params=_params("arbitrary", "arbitrary"),
        name="proj2",
    )(x, wt)


def _cache_layout_kernel(*refs, n_heads):
    x_ref, o_ref = refs[0], refs[-1]
    tm = x_ref.shape[0]
    for h in range(n_heads):
        o_ref[pl.ds(h, tm, stride=n_heads), :] = x_ref[:, h * HEAD_DIM:(h + 1) * HEAD_DIM]


def to_cache_layout(x, row0, rows, n_heads, slot, n_slots, into=None, tm=256):
    tm = _tile(math.gcd(rows, row0) if row0 else rows, tm, 8)
    r0, nt = row0 // tm, rows // tm
    alias = {} if into is None else dict(input_output_aliases={1: 0})
    extra_specs = [] if into is None else [pl.BlockSpec(memory_space=pl.ANY)]
    extra_args = [] if into is None else [into]
    return pl.pallas_call(
        functools.partial(_cache_layout_kernel, n_heads=n_heads),
        grid=(nt,),
        in_specs=[pl.BlockSpec((tm, n_heads * HEAD_DIM), lambda i: (r0 + i, 0)), *extra_specs],
        out_specs=pl.BlockSpec((tm * n_heads, HEAD_DIM), lambda i: (slot * nt + i, 0)),
        out_shape=jax.ShapeDtypeStruct((n_slots * rows * n_heads, HEAD_DIM), x.dtype),
        compiler_params=_params("parallel"),
        name="to_cache_layout",
        **alias,
    )(x, *extra_args)


def _out_proj_kernel(a_ref, c_ref, wa_ref, wc_ref, x_ref, o_ref, wab_s, wcb_s):
    _cast_on_first_row_tile([wa_ref, wc_ref], [wab_s, wcb_s])
    o_ref[...] = x_ref[...] + _dot(a_ref[...], wab_s[...]) + _dot(c_ref[...], wcb_s[...])


def out_proj(att, conv, w_out, layer, x, tm=1024, tn=512):
    m, ka = att.shape
    kc = conv.shape[1]
    n = w_out.shape[2]
    assert ka == kc
    tm, tn = _tile(m, tm, 8), _tile(n, tn, LANES)
    return pl.pallas_call(
        _out_proj_kernel,
        grid=(n // tn, m // tm),
        in_specs=[pl.BlockSpec((tm, ka), lambda j, i: (i, 0)),
                  pl.BlockSpec((tm, kc), lambda j, i: (i, 0)),
                  pl.BlockSpec((None, ka, tn), lambda j, i: (layer, 0, j)),
                  pl.BlockSpec((None, kc, tn), lambda j, i: (layer, 1, j)),
                  pl.BlockSpec((tm, tn), lambda j, i: (i, j))],
        out_specs=pl.BlockSpec((tm, tn), lambda j, i: (i, j)),
        out_shape=jax.ShapeDtypeStruct((m, n), F32),
        scratch_shapes=[pltpu.VMEM((ka, tn), BF16), pltpu.VMEM((kc, tn), BF16)],
        compiler_params=_params("arbitrary", "arbitrary"),
        name="out_proj",
    )(att, conv, w_out, w_out, x)


def _ffn_up_kernel(h_ref, wg_ref, wu_ref, o_ref, wgb_s, wub_s):
    _cast_on_first_row_tile([wg_ref, wu_ref], [wgb_s, wub_s])
    h = h_ref[...]
    g = _dot(h, wgb_s[...])
    u = _dot(h, wub_s[...])
    o_ref[...] = (g * jax.nn.sigmoid(g) * u).astype(o_ref.dtype)


def ffn_up(h, w_gate, w_up, layer, tm=1536, tn=256):
    m, k = h.shape
    n = w_gate.shape[2]
    tm, tn = _tile(m, tm, 8), _tile(n, tn, LANES)
    wspec = pl.BlockSpec((None, k, tn), lambda j, i: (layer, 0, j))
    return pl.pallas_call(
        _ffn_up_kernel,
        grid=(n // tn, m // tm),
        in_specs=[pl.BlockSpec((tm, k), lambda j, i: (i, 0)), wspec, wspec],
        out_specs=pl.BlockSpec((tm, tn), lambda j, i: (i, j)),
        out_shape=jax.ShapeDtypeStruct((m, n), BF16),
        scratch_shapes=[pltpu.VMEM((k, tn), BF16), pltpu.VMEM((k, tn), BF16)],
        compiler_params=_params("arbitrary", "arbitrary"),
        name="ffn_up",
    )(h, w_gate, w_up)


def _ffn_down_kernel(h_ref, w_ref, x_ref, o_ref):
    o_ref[...] = x_ref[...] + _dot(h_ref[...], w_ref[...])


def ffn_down(hidden, wd, layer, x, tm=512, tn=512):
    m, k = hidden.shape
    n = wd.shape[2]
    tm, tn = _tile(m, tm, 8), _tile(n, tn, LANES)
    return pl.pallas_call(
        _ffn_down_kernel,
        grid=(m // tm, n // tn),
        in_specs=[pl.BlockSpec((tm, k), lambda i, j: (i, 0)),
                  pl.BlockSpec((None, k, tn), lambda i, j: (layer, 0, j)),
                  pl.BlockSpec((tm, tn), lambda i, j: (i, j))],
        out_specs=pl.BlockSpec((tm, tn), lambda i, j: (i, j)),
        out_shape=jax.ShapeDtypeStruct((m, n), F32),
        compiler_params=_params("parallel", "parallel"),
        name="ffn_down",
    )(hidden, wd, x)


def _causal(s):
    row = lax.broadcasted_iota(jnp.int32, s.shape, 0)
    col = lax.broadcasted_iota(jnp.int32, s.shape, 1)
    return jnp.where(col <= row, s, NEG)


def _column(x, h):
    lane = lax.broadcasted_iota(jnp.int32, x.shape, 1)
    return jnp.sum(jnp.where(lane == h, x, 0.0), axis=-1, keepdims=True)


def _split3(x):
    hi = x.astype(BF16).astype(F32)
    r = x - hi
    mid = r.astype(BF16).astype(F32)
    lo = (r - mid).astype(BF16).astype(F32)
    return hi, mid, lo


def _bias_lanes(d, query_side):
    hi, mid, lo = _split3(d)
    lane = lax.broadcasted_iota(jnp.int32, (d.shape[0], LANES), 1)
    one = jnp.where(lane < 6, 1.0, 0.0)
    if query_side:
        v = jnp.where(lane == 0, hi, jnp.where(lane == 1, mid, jnp.where(lane == 2, lo, one)))
    else:
        v = jnp.where(lane == 3, -hi, jnp.where(lane == 4, -mid, jnp.where(lane == 5, -lo, one)))
    return v.astype(BF16)


def _flash_kernel(q_ref, k_ref, v_ref, c_ref, o_ref, qa_s, qb_s, ka_s, va_s, m_s, acc_s, *, tq, halves):
    h = pl.program_id(1)
    qi = pl.program_id(2)
    hd = HEAD_DIM

    @pl.when(qi == 0)
    def _():
        d = _column(c_ref[...], h) * LOG2E
        ka_s[:, :hd] = k_ref[...]
        ka_s[:, hd:] = _bias_lanes(d, False)
        qb_s[...] = _bias_lanes(d, True)
        va_s[:, :hd] = v_ref[...]
        lane = lax.broadcasted_iota(jnp.int32, (va_s.shape[0], LANES), 1)
        va_s[:, hd:] = jnp.where(lane == 0, 1.0, 0.0).astype(BF16)

    qa_s[:, :hd] = q_ref[...]
    qa_s[:, hd:] = qb_s[pl.ds(pl.multiple_of(qi * tq, tq), tq), :]
    m_s[...] = jnp.full(m_s.shape, NEG, F32)
    acc_s[...] = jnp.zeros(acc_s.shape, F32)
    th = tq // halves

    def scores(ki):
        k0 = pl.multiple_of(ki * tq, tq)
        ka = ka_s[pl.ds(k0, tq), :]
        return [_dot_nt(qa_s[r * th:(r + 1) * th, :], ka) for r in range(halves)]

    def update(ki, s_halves, masked):
        k0 = pl.multiple_of(ki * tq, tq)
        va = va_s[pl.ds(k0, tq), :]
        for r, s in enumerate(s_halves):
            rows = slice(r * th, (r + 1) * th)
            if masked:
                row = lax.broadcasted_iota(jnp.int32, s.shape, 0) + r * th
                col = lax.broadcasted_iota(jnp.int32, s.shape, 1)
                s = jnp.where(col <= row, s, NEG)
            m_prev = m_s[rows, :]
            m_new = jnp.maximum(m_prev, jnp.max(s, axis=-1, keepdims=True))
            p = jnp.exp2(s - m_new).astype(BF16)
            acc_s[rows, :] = jnp.exp2(m_prev - m_new) * acc_s[rows, :] + _dot(p, va)
            m_s[rows, :] = m_new

    def body(ki, s_halves):
        s_next = scores(ki + 1)
        update(ki, s_halves, False)
        return s_next

    s_last = lax.fori_loop(0, qi, body, scores(0))
    update(qi, s_last, True)
    acc = acc_s[...]
    o_ref[...] = acc[:, :hd] / acc[:, hd:hd + 1]


def flash_attention(q, k, v, cum, n_batch, seq, n_heads, tq=512):
    tq = _tile(seq, tq, LANES)
    nq = seq // tq
    hd = HEAD_DIM
    kernel = functools.partial(_flash_kernel, tq=tq, halves=2)
    return pl.pallas_call(
        kernel,
        grid=(n_batch, n_heads, nq),
        in_specs=[pl.BlockSpec((tq, hd), lambda b, h, qi: (b * nq + qi, h)),
                  pl.BlockSpec((seq, hd), lambda b, h, qi: (b, h)),
                  pl.BlockSpec((seq, hd), lambda b, h, qi: (b, h)),
                  pl.BlockSpec((seq, n_heads), lambda b, h, qi: (b, 0))],
        out_specs=pl.BlockSpec((tq, hd), lambda b, h, qi: (b * nq + qi, h)),
        out_shape=jax.ShapeDtypeStruct((n_batch * seq, n_heads * hd), F32),
        scratch_shapes=[pltpu.VMEM((tq, 2 * hd), BF16),
                        pltpu.VMEM((seq, hd), BF16),
                        pltpu.VMEM((seq, 2 * hd), BF16),
                        pltpu.VMEM((seq, 2 * hd), BF16),
                        pltpu.VMEM((tq, 1), F32),
                        pltpu.VMEM((tq, 2 * hd), F32)],
        compiler_params=_params("parallel", "parallel", "arbitrary"),
        name="flash_attention",
    )(q, k, v, cum)


def _decode_kernel(q_ref, kn_ref, vn_ref, kc_ref, vc_ref, cq_ref, ckc_ref, ckn_ref, g_ref, into_ref, o_ref,
                   m_s, l_s, acc_s, *, n_heads, t_new, tk):
    del into_ref
    kb = pl.program_id(1)
    hd = HEAD_DIM
    hps = n_heads // SUBLANES
    n_rows = tk * hps

    @pl.when(kb == 0)
    def _():
        m_s[...] = jnp.full(m_s.shape, NEG, F32)
        l_s[...] = jnp.zeros(l_s.shape, F32)
        acc_s[...] = jnp.zeros(acc_s.shape, F32)

    def online_update(rows, p, s, v):
        cols = slice(p * hd, (p + 1) * hd)
        m_prev = m_s[p, rows, :]
        m_new = jnp.maximum(m_prev, jnp.max(s, axis=-1, keepdims=True))
        alpha = jnp.exp2(m_prev - m_new)
        e = jnp.exp2(s - m_new)
        l_s[p, rows, :] = alpha * l_s[p, rows, :] + jnp.sum(e, axis=-1, keepdims=True)
        acc_s[rows, cols] = alpha * acc_s[rows, cols] + _dot(e.astype(BF16), v)
        m_s[p, rows, :] = m_new

    if hps > 1:
        shape = (hps * t_new, n_rows)
        row = lax.broadcasted_iota(jnp.int32, shape, 0)
        col = lax.broadcasted_iota(jnp.int32, shape, 1)
        row_head = sum((row >= r * t_new).astype(jnp.int32) for r in range(1, hps))
        own_head = row_head == (col & (hps - 1))

    heads_of = lambda p: [p + SUBLANES * r for r in range(hps)]
    for p in range(SUBLANES):
        q = jnp.concatenate([q_ref[:, h * hd:(h + 1) * hd] for h in heads_of(p)], axis=0)
        dq = jnp.concatenate([cq_ref[:, h:h + 1] for h in heads_of(p)], axis=0)
        k = kc_ref[pl.ds(p, n_rows, stride=SUBLANES), :].astype(BF16)
        v = vc_ref[pl.ds(p, n_rows, stride=SUBLANES), :].astype(BF16)
        s = _dot_nt(q, k) + (dq - ckc_ref[p:p + 1, :])
        if hps > 1:
            s = jnp.where(own_head, s, NEG)
        online_update(slice(0, hps * t_new), p, s, v)

    @pl.when(kb == pl.num_programs(1) - 1)
    def _():
        for h in range(n_heads):
            p, r = h % SUBLANES, h // SUBLANES
            cols = slice(h * hd, (h + 1) * hd)
            s = _dot_nt(q_ref[:, cols], kn_ref[:, cols]) + (cq_ref[:, h:h + 1] - ckn_ref[h:h + 1, :])
            online_update(slice(r * t_new, (r + 1) * t_new), p, _causal(s), vn_ref[:, cols])
        for p in range(SUBLANES):
            cols = slice(p * hd, (p + 1) * hd)
            acc_s[:, cols] = acc_s[:, cols] / l_s[p]
        att = jnp.concatenate([acc_s[r * t_new:(r + 1) * t_new, :] for r in range(hps)], axis=1)
        o_ref[...] = (att * _rms_scale(att) * g_ref[...]).astype(o_ref.dtype)


def decode_attention(q, k_new, v_new, row0, cache_k, cache_v, layer, cum_new, cum_cache_s, cum_new_t, g_att,
                     into, n_batch, t_new, n_heads, tk=512):
    width = n_heads * HEAD_DIM
    past = cache_k.shape[2] // n_heads
    hps = n_heads // SUBLANES
    assert n_heads % SUBLANES == 0 and hps & (hps - 1) == 0
    tk = _tile(past, tk, LANES)
    r0 = row0 // t_new
    kernel = functools.partial(_decode_kernel, n_heads=n_heads, t_new=t_new, tk=tk)
    new_spec = pl.BlockSpec((t_new, width), lambda b, kb: (r0 + b, 0))
    cache_spec = pl.BlockSpec((None, None, tk * n_heads, HEAD_DIM), lambda b, kb: (layer, b, kb, 0))
    return pl.pallas_call(
        kernel,
        grid=(n_batch, past // tk),
        in_specs=[new_spec, new_spec, new_spec, cache_spec, cache_spec,
                  pl.BlockSpec((t_new, n_heads), lambda b, kb: (b, 0)),
                  pl.BlockSpec((None, None, SUBLANES, tk * hps), lambda b, kb: (layer, b, 0, kb)),
                  pl.BlockSpec((None, n_heads, t_new), lambda b, kb: (b, 0, 0)),
                  pl.BlockSpec((1, width), lambda b, kb: (0, 0)),
                  pl.BlockSpec(memory_space=pl.ANY)],
        out_specs=pl.BlockSpec((t_new, width), lambda b, kb: (r0 + b, 0)),
        out_shape=jax.ShapeDtypeStruct(into.shape, into.dtype),
        input_output_aliases={9: 0},
        scratch_shapes=[pltpu.VMEM((SUBLANES, hps * t_new, 1), F32),
                        pltpu.VMEM((SUBLANES, hps * t_new, 1), F32),
                        pltpu.VMEM((hps * t_new, SUBLANES * HEAD_DIM), F32)],
        compiler_params=_params("parallel", "arbitrary"),
        name="decode_attention",
    )(q, k_new, v_new, cache_k, cache_v, cum_new, cum_cache_s, cum_new_t, g_att, into)


def _conv_kernel(*refs, tt):
    a_ref, g_ref, pre_ref, w_ref, b_ref, lng_ref, lnb_ref, gc_ref = refs[:8]
    o_ref, st_ref, u_s, y_s = refs[-4:]
    t = pl.program_id(1)
    n_slabs = u_s.shape[0]
    slab = lambda c: slice(c * LANES, (c + 1) * LANES)

    @pl.when(t == 0)
    def _():
        for c in range(n_slabs):
            u_s[c, 0:HALO, :] = pre_ref[:, slab(c)]

    @pl.when(t > 0)
    def _():
        for c in range(n_slabs):
            u_s[c, 0:HALO, :] = u_s[c, tt:tt + HALO, :]

    u = a_ref[...] * jax.nn.sigmoid(g_ref[...])
    for c in range(n_slabs):
        u_s[c, HALO:HALO + tt, :] = u[:, slab(c)]

    first = HALO - (CONV_WIDTH - 1)

    def group(gi, carry):
        r0 = pl.multiple_of(gi * CONV_GROUP, CONV_GROUP)
        for c in range(n_slabs):
            taps = [w_ref[j:j + 1, slab(c)] for j in range(CONV_WIDTH)]
            win = [u_s[c, pl.ds(r0 + first + k, SUBLANES, stride=CONV_ROW_STRIDE), :]
                   for k in range(CONV_WIDTH + CONV_ROW_STRIDE - 1)]
            for ph in range(CONV_ROW_STRIDE):
                acc = jnp.broadcast_to(b_ref[:, slab(c)], (SUBLANES, LANES))
                for j in range(CONV_WIDTH):
                    acc = acc + win[j + ph] * taps[j]
                y_s[c, pl.ds(r0 + ph, SUBLANES, stride=CONV_ROW_STRIDE), :] = acc
        return carry

    lax.fori_loop(0, tt // CONV_GROUP, group, 0)

    y = jnp.concatenate([y_s[c] for c in range(n_slabs)], axis=1)
    mu = jnp.mean(y, axis=-1, keepdims=True)
    yc = y - mu
    y = yc * lax.rsqrt(jnp.mean(yc * yc, axis=-1, keepdims=True) + EPS) * lng_ref[...] + lnb_ref[...]
    y = y * jax.nn.sigmoid(y)
    o_ref[...] = (y * _rms_scale(y) * gc_ref[...]).astype(o_ref.dtype)

    @pl.when(t == pl.num_programs(1) - 1)
    def _():
        st_ref[...] = jnp.concatenate([u_s[c, tt + first:tt + HALO, :] for c in range(n_slabs)], axis=1)


def conv_branch(glu, row0, prefix, w_dw, b_dw, cln_g, cln_b, g_conv, n_seq, seq, out_rows, into=None, tt=256):
    width = w_dw.shape[1]
    tt = _tile(math.gcd(seq, row0) if row0 else seq, tt, CONV_GROUP)
    assert tt % CONV_GROUP == 0 and width % LANES == 0
    nt = seq // tt
    r0 = row0 // tt
    kernel = functools.partial(_conv_kernel, tt=tt)
    vec = lambda n: pl.BlockSpec((n, width), lambda s, t: (0, 0))
    alias = {} if into is None else dict(input_output_aliases={8: 0})
    extra_specs = [] if into is None else [pl.BlockSpec(memory_space=pl.ANY)]
    extra_args = [] if into is None else [into]
    return pl.pallas_call(
        kernel,
        grid=(n_seq, nt),
        in_specs=[pl.BlockSpec((tt, width), lambda s, t: (r0 + s * nt + t, 0)),
                  pl.BlockSpec((tt, width), lambda s, t: (r0 + s * nt + t, 1)),
                  pl.BlockSpec((None, HALO, width), lambda s, t: (s, 0, 0)),
                  vec(CONV_WIDTH), vec(1), vec(1), vec(1), vec(1), *extra_specs],
        out_specs=[pl.BlockSpec((tt, width), lambda s, t: (r0 + s * nt + t, 0)),
                   pl.BlockSpec((None, CONV_WIDTH - 1, width), lambda s, t: (s, 0, 0))],
        out_shape=[jax.ShapeDtypeStruct((out_rows, width), BF16),
                   jax.ShapeDtypeStruct((n_seq, CONV_WIDTH - 1, width), F32)],
        scratch_shapes=[pltpu.VMEM((width // LANES, tt + HALO, LANES), F32),
                        pltpu.VMEM((width // LANES, tt, LANES), F32)],
        compiler_params=_params("parallel", "arbitrary"),
        name="conv_branch",
        **alias,
    )(glu, glu, prefix, w_dw, b_dw, cln_g, cln_b, g_conv, *extra_args)


def kernel(x_prompt, x_sample, cache_k, cache_v, cache_logf, state_conv, ln_mix, w_in, b_f,
           w_dw, b_dw, cln_g, cln_b, g_att, g_conv, w_out, ln_ffn, w_gate, w_up, w_down, g_final):
    n_b, seq, d = x_prompt.shape
    n_s, t_new, _ = x_sample.shape
    depth, _, past, n_heads, hd = cache_k.shape
    assert hd == HEAD_DIM
    att_w = n_heads * hd
    conv_w = w_dw.shape[2]
    assert conv_w == att_w and w_in.shape[2] == 3 * att_w + n_heads + 2 * conv_w
    mp = n_b * seq
    ms = n_s * t_new
    row = lambda v: v.reshape(1, -1)
    q_scale = hd ** -0.5 * LOG2E

    x = jnp.concatenate([x_prompt.reshape(mp, d), x_sample.reshape(ms, d)], axis=0)
    cum_cache = cumsum_time(cache_logf.reshape(depth * n_s, past, n_heads)).reshape(depth, n_s, past, n_heads)
    hps = n_heads // SUBLANES
    cum_cache_s = jnp.swapaxes(cum_cache.reshape(depth, n_s, past * hps, SUBLANES), 2, 3) * LOG2E
    ck = cache_k.reshape(depth, n_s, past * n_heads, hd)
    cv = cache_v.reshape(depth, n_s, past * n_heads, hd)
    zero_prefix = jnp.zeros((n_b, HALO, conv_w), F32)
    gate0 = 3 * att_w
    w_in_t = jnp.swapaxes(w_in, 1, 2)
    w_glu_t = w_in_t[:, gate0 + n_heads:, :]
    w_down_b = w_down.astype(BF16)

    outs = {k: [] for k in ("fp", "cp", "fs", "cs")}
    k_p = v_p = k_s = v_s = None
    for l in range(depth):
        h, logf = norm_gate(x, row(ln_mix[l]), w_in_t, l, gate0, row(b_f[l]), n_heads)
        q = proj(h, w_in_t, l, 0, att_w, BF16, out_scale=q_scale)
        k, kb = proj2(h, w_in_t, l, att_w, att_w)
        v, vb = proj2(h, w_in_t, l, 2 * att_w, att_w)
        glu = proj(h, w_glu_t, l, 0, 2 * conv_w, F32)
        k_p = to_cache_layout(k, 0, mp, n_heads, l, depth, into=k_p)
        v_p = to_cache_layout(v, 0, mp, n_heads, l, depth, into=v_p)
        k_s = to_cache_layout(k, mp, ms, n_heads, l, depth, into=k_s)
        v_s = to_cache_layout(v, mp, ms, n_heads, l, depth, into=v_s)

        logf_p = logf[:mp].reshape(n_b, seq, n_heads)
        logf_s = logf[mp:].reshape(n_s, t_new, n_heads)
        cum_p = cumsum_time(logf_p)
        cum_s = cumsum_time(logf_s, offset=cum_cache[l][:, -1, :])

        att_p = flash_attention(q, kb, vb, cum_p.reshape(mp, n_heads), n_b, seq, n_heads)
        att = rmsnorm(att_p, row(g_att[l]), BF16, out_rows=mp + ms)
        cum_s2 = cum_s * LOG2E
        att = decode_attention(q, kb, vb, mp, ck, cv, l, cum_s2.reshape(ms, n_heads), cum_cache_s,
                               jnp.swapaxes(cum_s2, 1, 2), row(g_att[l]), att, n_s, t_new, n_heads)

        conv_args = (w_dw[l], row(b_dw[l]), row(cln_g[l]), row(cln_b[l]), row(g_conv[l]))
        conv, st_p = conv_branch(glu, 0, zero_prefix, *conv_args, n_b, seq, mp + ms)
        prefix_s = jnp.pad(state_conv[l], ((0, 0), (HALO - (CONV_WIDTH - 1), 0), (0, 0)))
        conv, st_s = conv_branch(glu, mp, prefix_s, *conv_args, n_s, t_new, mp + ms, into=conv)

        x = out_proj(att, conv, w_out, l, x)

        h2 = rmsnorm(x, row(ln_ffn[l]), BF16)
        hidden = ffn_up(h2, w_gate, w_up, l)
        x = ffn_down(hidden, w_down_b, l, x)

        outs["fp"].append(logf_p)
        outs["cp"].append(st_p)
        outs["fs"].append(logf_s)
        outs["cs"].append(st_s)

    y_p = rmsnorm(x, row(g_final), F32, 0, mp)
    y_s = rmsnorm(x, row(g_final), F32, mp, ms)
    stk = lambda k: jnp.stack(outs[k])
    prompt_kv = lambda a: a.reshape(depth, n_b, seq, n_heads, hd)
    sample_kv = lambda a: a.reshape(depth, n_s, t_new, n_heads, hd)
    return (y_p.reshape(n_b, seq, d), y_s.reshape(n_s, t_new, d),
            prompt_kv(k_p), prompt_kv(v_p), stk("fp"), stk("cp"),
            sample_kv(k_s), sample_kv(v_s), stk("fs"), stk("cs"))
```

```python
import functools
import math

import jax
import jax.numpy as jnp
from jax import lax
from jax.experimental import pallas as pl
from jax.experimental.pallas import tpu as pltpu

HEAD_DIM = 128
CONV_WIDTH = 31
HALO = 32
EPS = 1e-6
NEG = -1e30
LOG2E = math.log2(math.e)
LANES = 128
SUBLANES = 8
CONV_ROW_STRIDE = 4
CONV_GROUP = SUBLANES * CONV_ROW_STRIDE
VMEM_LIMIT = 60 * 1024 * 1024

BF16 = jnp.bfloat16
F32 = jnp.float32


def _tile(dim, pref, mult):
    t = min(pref, dim)
    t -= t % mult
    while t >= mult:
        if dim % t == 0:
            return t
        t -= mult
    return dim


def _params(*sem):
    return pltpu.CompilerParams(dimension_semantics=sem, vmem_limit_bytes=VMEM_LIMIT)


def _dot(a, b):
    return jnp.dot(a, b, preferred_element_type=F32)


def _dot_nt(a, b):
    return lax.dot_general(a, b, (((1,), (1,)), ((), ())), preferred_element_type=F32)


def _log_sigmoid(x):
    return jnp.minimum(x, 0.0) - jnp.log1p(jnp.exp(-jnp.abs(x)))


def _rms_scale(x):
    return lax.rsqrt(jnp.mean(x * x, axis=-1, keepdims=True) + EPS)


def _norm_gate_kernel(x_ref, g_ref, wf_ref, bf_ref, h_ref, lf_ref):
    x = x_ref[...]
    h = (x * _rms_scale(x) * g_ref[...]).astype(BF16)
    h_ref[...] = h
    lf_ref[...] = _log_sigmoid(_dot_nt(h, wf_ref[...].astype(BF16)) + bf_ref[...])


def norm_gate(x, g, wt, layer, row0, bf, n_heads):
    m, d = x.shape
    tm = _tile(m, 256, 8)
    assert row0 % n_heads == 0 and n_heads % SUBLANES == 0
    return pl.pallas_call(
        _norm_gate_kernel,
        grid=(m // tm,),
        in_specs=[pl.BlockSpec((tm, d), lambda i: (i, 0)),
                  pl.BlockSpec((1, d), lambda i: (0, 0)),
                  pl.BlockSpec((None, n_heads, d), lambda i: (layer, row0 // n_heads, 0)),
                  pl.BlockSpec((1, n_heads), lambda i: (0, 0))],
        out_specs=[pl.BlockSpec((tm, d), lambda i: (i, 0)),
                   pl.BlockSpec((tm, n_heads), lambda i: (i, 0))],
        out_shape=[jax.ShapeDtypeStruct((m, d), BF16),
                   jax.ShapeDtypeStruct((m, n_heads), F32)],
        compiler_params=_params("parallel"),
        name="norm_gate",
    )(x, g, wt, bf)


def _rmsnorm_kernel(x_ref, g_ref, o_ref):
    x = x_ref[...]
    o_ref[...] = (x * _rms_scale(x) * g_ref[...]).astype(o_ref.dtype)


def rmsnorm(x, g, out_dtype, row0=0, rows=None, out_rows=None):
    d = x.shape[1]
    rows = x.shape[0] if rows is None else rows
    out_rows = rows if out_rows is None else out_rows
    tm = _tile(math.gcd(rows, row0) if row0 else rows, 256, 8)
    r0 = row0 // tm
    return pl.pallas_call(
        _rmsnorm_kernel,
        grid=(rows // tm,),
        in_specs=[pl.BlockSpec((tm, d), lambda i: (r0 + i, 0)),
                  pl.BlockSpec((1, d), lambda i: (0, 0))],
        out_specs=pl.BlockSpec((tm, d), lambda i: (i, 0)),
        out_shape=jax.ShapeDtypeStruct((out_rows, d), out_dtype),
        compiler_params=_params("parallel"),
        name="rmsnorm",
    )(x, g)


def _cumsum_kernel(x_ref, off_ref, o_ref, *, rows_per_seq, n_heads):
    x = x_ref[...]
    tb = x.shape[0]
    head_bits = n_heads.bit_length() - 1
    seq_bits = rows_per_seq.bit_length() - 1
    li = lax.broadcasted_iota(jnp.int32, (LANES, LANES), 0)
    lj = lax.broadcasted_iota(jnp.int32, (LANES, LANES), 1)
    same_head = (li & (n_heads - 1)) == (lj & (n_heads - 1))
    upper = (same_head & ((li >> head_bits) <= (lj >> head_bits))).astype(F32)
    every = same_head.astype(F32)
    hi = lax.Precision.HIGHEST
    in_row = jnp.dot(x, upper, precision=hi, preferred_element_type=F32)
    row_tot = jnp.dot(x, every, precision=hi, preferred_element_type=F32)
    ri = lax.broadcasted_iota(jnp.int32, (tb, tb), 0)
    rj = lax.broadcasted_iota(jnp.int32, (tb, tb), 1)
    earlier = (((ri >> seq_bits) == (rj >> seq_bits)) & (rj < ri)).astype(F32)
    before = jnp.dot(earlier, row_tot, precision=hi, preferred_element_type=F32)
    o_ref[...] = in_row + before + off_ref[...]


def cumsum_time(logf, offset=None):
    b, t, h = logf.shape
    per_row = LANES // h
    rows = t // per_row
    assert h & (h - 1) == 0 and rows & (rows - 1) == 0 and rows * per_row == t
    x = logf.reshape(b * rows, LANES)
    if offset is None:
        off = jnp.zeros((b * rows, LANES), F32)
    else:
        off = jnp.broadcast_to(offset[:, None, None, :], (b, rows, per_row, h)).reshape(b * rows, LANES)
    tb = rows * _tile(b, max(1, 512 // rows), 1)
    out = pl.pallas_call(
        functools.partial(_cumsum_kernel, rows_per_seq=rows, n_heads=h),
        grid=(b * rows // tb,),
        in_specs=[pl.BlockSpec((tb, LANES), lambda i: (i, 0)),
                  pl.BlockSpec((tb, LANES), lambda i: (i, 0))],
        out_specs=pl.BlockSpec((tb, LANES), lambda i: (i, 0)),
        out_shape=jax.ShapeDtypeStruct((b * rows, LANES), F32),
        compiler_params=_params("parallel"),
        name="cumsum_time",
    )(x, off)
    return out.reshape(b, t, h)


def _cast_on_first_row_tile(w_refs, wb_refs):
    @pl.when(pl.program_id(1) == 0)
    def _():
        for w_ref, wb_ref in zip(w_refs, wb_refs):
            wb_ref[...] = w_ref[...].astype(BF16)


def _proj_kernel(x_ref, w_ref, o_ref, wb_s, *, out_scale):
    _cast_on_first_row_tile([w_ref], [wb_s])
    acc = _dot_nt(x_ref[...], wb_s[...])
    if out_scale != 1.0:
        acc = acc * out_scale
    o_ref[...] = acc.astype(o_ref.dtype)


def proj(x, wt, layer, row0, n, out_dtype, out_scale=1.0, tm=1536, tn=512):
    m, k = x.shape
    tm, tn = _tile(m, tm, 8), _tile(math.gcd(n, row0) if row0 else n, tn, LANES)
    r0 = row0 // tn
    return pl.pallas_call(
        functools.partial(_proj_kernel, out_scale=out_scale),
        grid=(n // tn, m // tm),
        in_specs=[pl.BlockSpec((tm, k), lambda j, i: (i, 0)),
                  pl.BlockSpec((None, tn, k), lambda j, i: (layer, r0 + j, 0))],
        out_specs=pl.BlockSpec((tm, tn), lambda j, i: (i, j)),
        out_shape=jax.ShapeDtypeStruct((m, n), out_dtype),
        scratch_shapes=[pltpu.VMEM((tn, k), BF16)],
        compiler_params=_params("arbitrary", "arbitrary"),
        name="proj",
    )(x, wt)


def _proj2_kernel(x_ref, w_ref, o_ref, ob_ref, wb_s):
    _cast_on_first_row_tile([w_ref], [wb_s])
    acc = _dot_nt(x_ref[...], wb_s[...])
    o_ref[...] = acc
    ob_ref[...] = acc.astype(BF16)


def proj2(x, wt, layer, row0, n, tm=1536, tn=512):
    m, k = x.shape
    tm, tn = _tile(m, tm, 8), _tile(math.gcd(n, row0) if row0 else n, tn, LANES)
    r0 = row0 // tn
    out = pl.BlockSpec((tm, tn), lambda j, i: (i, j))
    return pl.pallas_call(
        _proj2_kernel,
        grid=(n // tn, m // tm),
        in_specs=[pl.BlockSpec((tm, k), lambda j, i: (i, 0)),
                  pl.BlockSpec((None, tn, k), lambda j, i: (layer, r0 + j, 0))],
        out_specs=[out, out],
        out_shape=[jax.ShapeDtypeStruct((m, n), F32), jax.ShapeDtypeStruct((m, n), BF16)],
        scratch_shapes=[pltpu.VMEM((tn, k), BF16)],
        compiler_params=_params("arbitrary", "arbitrary"),
        name="proj2",
    )(x, wt)


def _cache_layout_kernel(*refs, n_heads):
    x_ref, o_ref = refs[0], refs[-1]
    tm = x_ref.shape[0]
    for h in range(n_heads):
        o_ref[pl.ds(h, tm, stride=n_heads), :] = x_ref[:, h * HEAD_DIM:(h + 1) * HEAD_DIM]


def to_cache_layout(x, row0, rows, n_heads, slot, n_slots, into=None, tm=256):
    tm = _tile(math.gcd(rows, row0) if row0 else rows, tm, 8)
    r0, nt = row0 // tm, rows // tm
    alias = {} if into is None else dict(input_output_aliases={1: 0})
    extra_specs = [] if into is None else [pl.BlockSpec(memory_space=pl.ANY)]
    extra_args = [] if into is None else [into]
    return pl.pallas_call(
        functools.partial(_cache_layout_kernel, n_heads=n_heads),
        grid=(nt,),
        in_specs=[pl.BlockSpec((tm, n_heads * HEAD_DIM), lambda i: (r0 + i, 0)), *extra_specs],
        out_specs=pl.BlockSpec((tm * n_heads, HEAD_DIM), lambda i: (slot * nt + i, 0)),
        out_shape=jax.ShapeDtypeStruct((n_slots * rows * n_heads, HEAD_DIM), x.dtype),
        compiler_params=_params("parallel"),
        name="to_cache_layout",
        **alias,
    )(x, *extra_args)


def _out_proj_kernel(a_ref, c_ref, wa_ref, wc_ref, x_ref, o_ref, wab_s, wcb_s):
    _cast_on_first_row_tile([wa_ref, wc_ref], [wab_s, wcb_s])
    o_ref[...] = x_ref[...] + _dot(a_ref[...], wab_s[...]) + _dot(c_ref[...], wcb_s[...])


def out_proj(att, conv, w_out, layer, x, tm=1536, tn=512):
    m, ka = att.shape
    kc = conv.shape[1]
    n = w_out.shape[2]
    assert ka == kc
    tm, tn = _tile(m, tm, 8), _tile(n, tn, LANES)
    return pl.pallas_call(
        _out_proj_kernel,
        grid=(n // tn, m // tm),
        in_specs=[pl.BlockSpec((tm, ka), lambda j, i: (i, 0)),
                  pl.BlockSpec((tm, kc), lambda j, i: (i, 0)),
                  pl.BlockSpec((None, ka, tn), lambda j, i: (layer, 0, j)),
                  pl.BlockSpec((None, kc, tn), lambda j, i: (layer, 1, j)),
                  pl.BlockSpec((tm, tn), lambda j, i: (i, j))],
        out_specs=pl.BlockSpec((tm, tn), lambda j, i: (i, j)),
        out_shape=jax.ShapeDtypeStruct((m, n), F32),
        scratch_shapes=[pltpu.VMEM((ka, tn), BF16), pltpu.VMEM((kc, tn), BF16)],
        compiler_params=_params("arbitrary", "arbitrary"),
        name="out_proj",
    )(att, conv, w_out, w_out, x)


def _ffn_up_kernel(h_ref, wg_ref, wu_ref, o_ref, wgb_s, wub_s):
    _cast_on_first_row_tile([wg_ref, wu_ref], [wgb_s, wub_s])
    h = h_ref[...]
    g = _dot(h, wgb_s[...])
    u = _dot(h, wub_s[...])
    o_ref[...] = (g * jax.nn.sigmoid(g) * u).astype(o_ref.dtype)


def ffn_up(h, w_gate, w_up, layer, tm=1536, tn=256):
    m, k = h.shape
    n = w_gate.shape[2]
    tm, tn = _tile(m, tm, 8), _tile(n, tn, LANES)
    wspec = pl.BlockSpec((None, k, tn), lambda j, i: (layer, 0, j))
    return pl.pallas_call(
        _ffn_up_kernel,
        grid=(n // tn, m // tm),
        in_specs=[pl.BlockSpec((tm, k), lambda j, i: (i, 0)), wspec, wspec],
        out_specs=pl.BlockSpec((tm, tn), lambda j, i: (i, j)),
        out_shape=jax.ShapeDtypeStruct((m, n), BF16),
        scratch_shapes=[pltpu.VMEM((k, tn), BF16), pltpu.VMEM((k, tn), BF16)],
        compiler_params=_params("arbitrary", "arbitrary"),
        name="ffn_up",
    )(h, w_gate, w_up)


def _ffn_down_kernel(h_ref, w_ref, x_ref, o_ref):
    o_ref[...] = x_ref[...] + _dot(h_ref[...], w_ref[...])


def ffn_down(hidden, wd, layer, x, tm=512, tn=512):
    m, k = hidden.shape
    n = wd.shape[2]
    tm, tn = _tile(m, tm, 8), _tile(n, tn, LANES)
    return pl.pallas_call(
        _ffn_down_kernel,
        grid=(m // tm, n // tn),
        in_specs=[pl.BlockSpec((tm, k), lambda i, j: (i, 0)),
                  pl.BlockSpec((None, k, tn), lambda i, j: (layer, 0, j)),
                  pl.BlockSpec((tm, tn), lambda i, j: (i, j))],
        out_specs=pl.BlockSpec((tm, tn), lambda i, j: (i, j)),
        out_shape=jax.ShapeDtypeStruct((m, n), F32),
        compiler_params=_params("parallel", "parallel"),
        name="ffn_down",
    )(hidden, wd, x)


def _causal(s):
    row = lax.broadcasted_iota(jnp.int32, s.shape, 0)
    col = lax.broadcasted_iota(jnp.int32, s.shape, 1)
    return jnp.where(col <= row, s, NEG)


def _column(x, h):
    lane = lax.broadcasted_iota(jnp.int32, x.shape, 1)
    return jnp.sum(jnp.where(lane == h, x, 0.0), axis=-1, keepdims=True)


def _split3(x):
    hi = x.astype(BF16).astype(F32)
    r = x - hi
    mid = r.astype(BF16).astype(F32)
    lo = (r - mid).astype(BF16).astype(F32)
    return hi, mid, lo


def _bias_lanes(d, query_side):
    hi, mid, lo = _split3(d)
    lane = lax.broadcasted_iota(jnp.int32, (d.shape[0], LANES), 1)
    one = jnp.where(lane < 6, 1.0, 0.0)
    if query_side:
        v = jnp.where(lane == 0, hi, jnp.where(lane == 1, mid, jnp.where(lane == 2, lo, one)))
    else:
        v = jnp.where(lane == 3, -hi, jnp.where(lane == 4, -mid, jnp.where(lane == 5, -lo, one)))
    return v.astype(BF16)


def _flash_kernel(q_ref, k_ref, v_ref, c_ref, o_ref, qa_s, qb_s, ka_s, va_s, m_s, acc_s, *, tq, halves):
    h = pl.program_id(1)
    qi = pl.program_id(2)
    hd = HEAD_DIM

    @pl.when(qi == 0)
    def _():
        d = _column(c_ref[...], h) * LOG2E
        ka_s[:, :hd] = k_ref[...]
        ka_s[:, hd:] = _bias_lanes(d, False)
        qb_s[...] = _bias_lanes(d, True)
        va_s[:, :hd] = v_ref[...]
        lane = lax.broadcasted_iota(jnp.int32, (va_s.shape[0], LANES), 1)
        va_s[:, hd:] = jnp.where(lane == 0, 1.0, 0.0).astype(BF16)

    qa_s[:, :hd] = q_ref[...]
    qa_s[:, hd:] = qb_s[pl.ds(pl.multiple_of(qi * tq, tq), tq), :]
    m_s[...] = jnp.full(m_s.shape, NEG, F32)
    acc_s[...] = jnp.zeros(acc_s.shape, F32)
    th = tq // halves

    def scores(ki):
        k0 = pl.multiple_of(ki * tq, tq)
        ka = ka_s[pl.ds(k0, tq), :]
        return [_dot_nt(qa_s[r * th:(r + 1) * th, :], ka) for r in range(halves)]

    def update(ki, s_halves, masked):
        k0 = pl.multiple_of(ki * tq, tq)
        va = va_s[pl.ds(k0, tq), :]
        for r, s in enumerate(s_halves):
            rows = slice(r * th, (r + 1) * th)
            if masked:
                row = lax.broadcasted_iota(jnp.int32, s.shape, 0) + r * th
                col = lax.broadcasted_iota(jnp.int32, s.shape, 1)
                s = jnp.where(col <= row, s, NEG)
            m_prev = m_s[rows, :]
            m_new = jnp.maximum(m_prev, jnp.max(s, axis=-1, keepdims=True))
            p = jnp.exp2(s - m_new).astype(BF16)
            acc_s[rows, :] = jnp.exp2(m_prev - m_new) * acc_s[rows, :] + _dot(p, va)
            m_s[rows, :] = m_new

    def body(ki, s_halves):
        s_next = scores(ki + 1)
        update(ki, s_halves, False)
        return s_next

    s_last = lax.fori_loop(0, qi, body, scores(0))
    update(qi, s_last, True)
    acc = acc_s[...]
    o_ref[...] = acc[:, :hd] / acc[:, hd:hd + 1]


def flash_attention(q, k, v, cum, n_batch, seq, n_heads, tq=1024):
    tq = _tile(seq, tq, LANES)
    nq = seq // tq
    hd = HEAD_DIM
    kernel = functools.partial(_flash_kernel, tq=tq, halves=4)
    return pl.pallas_call(
        kernel,
        grid=(n_batch, n_heads, nq),
        in_specs=[pl.BlockSpec((tq, hd), lambda b, h, qi: (b * nq + qi, h)),
                  pl.BlockSpec((seq, hd), lambda b, h, qi: (b, h)),
                  pl.BlockSpec((seq, hd), lambda b, h, qi: (b, h)),
                  pl.BlockSpec((seq, n_heads), lambda b, h, qi: (b, 0))],
        out_specs=pl.BlockSpec((tq, hd), lambda b, h, qi: (b * nq + qi, h)),
        out_shape=jax.ShapeDtypeStruct((n_batch * seq, n_heads * hd), F32),
        scratch_shapes=[pltpu.VMEM((tq, 2 * hd), BF16),
                        pltpu.VMEM((seq, hd), BF16),
                        pltpu.VMEM((seq, 2 * hd), BF16),
                        pltpu.VMEM((seq, 2 * hd), BF16),
                        pltpu.VMEM((tq, 1), F32),
                        pltpu.VMEM((tq, 2 * hd), F32)],
        compiler_params=_params("parallel", "parallel", "arbitrary"),
        name="flash_attention",
    )(q, k, v, cum)


def _decode_kernel(q_ref, kn_ref, vn_ref, kc_ref, vc_ref, cq_ref, ckc_ref, ckn_ref, g_ref, into_ref, o_ref,
                   m_s, l_s, acc_s, *, n_heads, t_new, tk):
    del into_ref
    kb = pl.program_id(1)
    hd = HEAD_DIM
    hps = n_heads // SUBLANES
    n_rows = tk * hps

    @pl.when(kb == 0)
    def _():
        m_s[...] = jnp.full(m_s.shape, NEG, F32)
        l_s[...] = jnp.zeros(l_s.shape, F32)
        acc_s[...] = jnp.zeros(acc_s.shape, F32)

    def online_update(rows, p, s, v):
        cols = slice(p * hd, (p + 1) * hd)
        m_prev = m_s[p, rows, :]
        m_new = jnp.maximum(m_prev, jnp.max(s, axis=-1, keepdims=True))
        alpha = jnp.exp2(m_prev - m_new)
        e = jnp.exp2(s - m_new)
        l_s[p, rows, :] = alpha * l_s[p, rows, :] + jnp.sum(e, axis=-1, keepdims=True)
        acc_s[rows, cols] = alpha * acc_s[rows, cols] + _dot(e.astype(BF16), v)
        m_s[p, rows, :] = m_new

    if hps > 1:
        shape = (hps * t_new, n_rows)
        row = lax.broadcasted_iota(jnp.int32, shape, 0)
        col = lax.broadcasted_iota(jnp.int32, shape, 1)
        row_head = sum((row >= r * t_new).astype(jnp.int32) for r in range(1, hps))
        own_head = row_head == (col & (hps - 1))

    heads_of = lambda p: [p + SUBLANES * r for r in range(hps)]
    for p in range(SUBLANES):
        q = jnp.concatenate([q_ref[:, h * hd:(h + 1) * hd] for h in heads_of(p)], axis=0)
        dq = jnp.concatenate([cq_ref[:, h:h + 1] for h in heads_of(p)], axis=0)
        k = kc_ref[pl.ds(p, n_rows, stride=SUBLANES), :].astype(BF16)
        v = vc_ref[pl.ds(p, n_rows, stride=SUBLANES), :].astype(BF16)
        s = _dot_nt(q, k) + (dq - ckc_ref[p:p + 1, :])
        if hps > 1:
            s = jnp.where(own_head, s, NEG)
        online_update(slice(0, hps * t_new), p, s, v)

    @pl.when(kb == pl.num_programs(1) - 1)
    def _():
        for h in range(n_heads):
            p, r = h % SUBLANES, h // SUBLANES
            cols = slice(h * hd, (h + 1) * hd)
            s = _dot_nt(q_ref[:, cols], kn_ref[:, cols]) + (cq_ref[:, h:h + 1] - ckn_ref[h:h + 1, :])
            online_update(slice(r * t_new, (r + 1) * t_new), p, _causal(s), vn_ref[:, cols])
        for p in range(SUBLANES):
            cols = slice(p * hd, (p + 1) * hd)
            acc_s[:, cols] = acc_s[:, cols] / l_s[p]
        att = jnp.concatenate([acc_s[r * t_new:(r + 1) * t_new, :] for r in range(hps)], axis=1)
        o_ref[...] = (att * _rms_scale(att) * g_ref[...]).astype(o_ref.dtype)


def decode_attention(q, k_new, v_new, row0, cache_k, cache_v, layer, cum_new, cum_cache_s, cum_new_t, g_att,
                     into, n_batch, t_new, n_heads, tk=1024):
    width = n_heads * HEAD_DIM
    past = cache_k.shape[2] // n_heads
    hps = n_heads // SUBLANES
    assert n_heads % SUBLANES == 0 and hps & (hps - 1) == 0
    tk = _tile(past, tk, LANES)
    r0 = row0 // t_new
    kernel = functools.partial(_decode_kernel, n_heads=n_heads, t_new=t_new, tk=tk)
    new_spec = pl.BlockSpec((t_new, width), lambda b, kb: (r0 + b, 0))
    cache_spec = pl.BlockSpec((None, None, tk * n_heads, HEAD_DIM), lambda b, kb: (layer, b, kb, 0))
    return pl.pallas_call(
        kernel,
        grid=(n_batch, past // tk),
        in_specs=[new_spec, new_spec, new_spec, cache_spec, cache_spec,
                  pl.BlockSpec((t_new, n_heads), lambda b, kb: (b, 0)),
                  pl.BlockSpec((None, None, SUBLANES, tk * hps), lambda b, kb: (layer, b, 0, kb)),
                  pl.BlockSpec((None, n_heads, t_new), lambda b, kb: (b, 0, 0)),
                  pl.BlockSpec((1, width), lambda b, kb: (0, 0)),
                  pl.BlockSpec(memory_space=pl.ANY)],
        out_specs=pl.BlockSpec((t_new, width), lambda b, kb: (r0 + b, 0)),
        out_shape=jax.ShapeDtypeStruct(into.shape, into.dtype),
        input_output_aliases={9: 0},
        scratch_shapes=[pltpu.VMEM((SUBLANES, hps * t_new, 1), F32),
                        pltpu.VMEM((SUBLANES, hps * t_new, 1), F32),
                        pltpu.VMEM((hps * t_new, SUBLANES * HEAD_DIM), F32)],
        compiler_params=_params("parallel", "arbitrary"),
        name="decode_attention",
    )(q, k_new, v_new, cache_k, cache_v, cum_new, cum_cache_s, cum_new_t, g_att, into)


def _conv_kernel(*refs, tt):
    a_ref, g_ref, pre_ref, w_ref, b_ref, lng_ref, lnb_ref, gc_ref = refs[:8]
    o_ref, st_ref, u_s, y_s = refs[-4:]
    t = pl.program_id(1)
    n_slabs = u_s.shape[0]
    slab = lambda c: slice(c * LANES, (c + 1) * LANES)

    @pl.when(t == 0)
    def _():
        for c in range(n_slabs):
            u_s[c, 0:HALO, :] = pre_ref[:, slab(c)]

    @pl.when(t > 0)
    def _():
        for c in range(n_slabs):
            u_s[c, 0:HALO, :] = u_s[c, tt:tt + HALO, :]

    u = a_ref[...] * jax.nn.sigmoid(g_ref[...])
    for c in range(n_slabs):
        u_s[c, HALO:HALO + tt, :] = u[:, slab(c)]

    first = HALO - (CONV_WIDTH - 1)

    def group(gi, carry):
        r0 = pl.multiple_of(gi * CONV_GROUP, CONV_GROUP)
        for c in range(n_slabs):
            taps = [w_ref[j:j + 1, slab(c)] for j in range(CONV_WIDTH)]
            win = [u_s[c, pl.ds(r0 + first + k, SUBLANES, stride=CONV_ROW_STRIDE), :]
                   for k in range(CONV_WIDTH + CONV_ROW_STRIDE - 1)]
            for ph in range(CONV_ROW_STRIDE):
                acc = jnp.broadcast_to(b_ref[:, slab(c)], (SUBLANES, LANES))
                for j in range(CONV_WIDTH):
                    acc = acc + win[j + ph] * taps[j]
                y_s[c, pl.ds(r0 + ph, SUBLANES, stride=CONV_ROW_STRIDE), :] = acc
        return carry

    lax.fori_loop(0, tt // CONV_GROUP, group, 0)

    y = jnp.concatenate([y_s[c] for c in range(n_slabs)], axis=1)
    mu = jnp.mean(y, axis=-1, keepdims=True)
    yc = y - mu
    y = yc * lax.rsqrt(jnp.mean(yc * yc, axis=-1, keepdims=True) + EPS) * lng_ref[...] + lnb_ref[...]
    y = y * jax.nn.sigmoid(y)
    o_ref[...] = (y * _rms_scale(y) * gc_ref[...]).astype(o_ref.dtype)

    @pl.when(t == pl.num_programs(1) - 1)
    def _():
        st_ref[...] = jnp.concatenate([u_s[c, tt + first:tt + HALO, :] for c in range(n_slabs)], axis=1)


def conv_branch(glu, row0, prefix, w_dw, b_dw, cln_g, cln_b, g_conv, n_seq, seq, out_rows, into=None, tt=256):
    width = w_dw.shape[1]
    tt = _tile(math.gcd(seq, row0) if row0 else seq, tt, CONV_GROUP)
    assert tt % CONV_GROUP == 0 and width % LANES == 0
    nt = seq // tt
    r0 = row0 // tt
    kernel = functools.partial(_conv_kernel, tt=tt)
    vec = lambda n: pl.BlockSpec((n, width), lambda s, t: (0, 0))
    alias = {} if into is None else dict(input_output_aliases={8: 0})
    extra_specs = [] if into is None else [pl.BlockSpec(memory_space=pl.ANY)]
    extra_args = [] if into is None else [into]
    return pl.pallas_call(
        kernel,
        grid=(n_seq, nt),
        in_specs=[pl.BlockSpec((tt, width), lambda s, t: (r0 + s * nt + t, 0)),
                  pl.BlockSpec((tt, width), lambda s, t: (r0 + s * nt + t, 1)),
                  pl.BlockSpec((None, HALO, width), lambda s, t: (s, 0, 0)),
                  vec(CONV_WIDTH), vec(1), vec(1), vec(1), vec(1), *extra_specs],
        out_specs=[pl.BlockSpec((tt, width), lambda s, t: (r0 + s * nt + t, 0)),
                   pl.BlockSpec((None, CONV_WIDTH - 1, width), lambda s, t: (s, 0, 0))],
        out_shape=[jax.ShapeDtypeStruct((out_rows, width), BF16),
                   jax.ShapeDtypeStruct((n_seq, CONV_WIDTH - 1, width), F32)],
        scratch_shapes=[pltpu.VMEM((width // LANES, tt + HALO, LANES), F32),
                        pltpu.VMEM((width // LANES, tt, LANES), F32)],
        compiler_params=_params("parallel", "arbitrary"),
        name="conv_branch",
        **alias,
    )(glu, glu, prefix, w_dw, b_dw, cln_g, cln_b, g_conv, *extra_args)


def kernel(x_prompt, x_sample, cache_k, cache_v, cache_logf, state_conv, ln_mix, w_in, b_f,
           w_dw, b_dw, cln_g, cln_b, g_att, g_conv, w_out, ln_ffn, w_gate, w_up, w_down, g_final):
    n_b, seq, d = x_prompt.shape
    n_s, t_new, _ = x_sample.shape
    depth, _, past, n_heads, hd = cache_k.shape
    assert hd == HEAD_DIM
    att_w = n_heads * hd
    conv_w = w_dw.shape[2]
    assert conv_w == att_w and w_in.shape[2] == 3 * att_w + n_heads + 2 * conv_w
    mp = n_b * seq
    ms = n_s * t_new
    row = lambda v: v.reshape(1, -1)
    q_scale = hd ** -0.5 * LOG2E

    x = jnp.concatenate([x_prompt.reshape(mp, d), x_sample.reshape(ms, d)], axis=0)
    cum_cache = cumsum_time(cache_logf.reshape(depth * n_s, past, n_heads)).reshape(depth, n_s, past, n_heads)
    hps = n_heads // SUBLANES
    cum_cache_s = jnp.swapaxes(cum_cache.reshape(depth, n_s, past * hps, SUBLANES), 2, 3) * LOG2E
    ck = cache_k.reshape(depth, n_s, past * n_heads, hd)
    cv = cache_v.reshape(depth, n_s, past * n_heads, hd)
    zero_prefix = jnp.zeros((n_b, HALO, conv_w), F32)
    gate0 = 3 * att_w
    w_in_t = jnp.swapaxes(w_in, 1, 2)
    w_glu_t = w_in_t[:, gate0 + n_heads:, :]
    w_down_b = w_down.astype(BF16)

    outs = {k: [] for k in ("fp", "cp", "fs", "cs")}
    k_p = v_p = k_s = v_s = None
    for l in range(depth):
        h, logf = norm_gate(x, row(ln_mix[l]), w_in_t, l, gate0, row(b_f[l]), n_heads)
        q = proj(h, w_in_t, l, 0, att_w, BF16, out_scale=q_scale)
        k, kb = proj2(h, w_in_t, l, att_w, att_w)
        v, vb = proj2(h, w_in_t, l, 2 * att_w, att_w)
        glu = proj(h, w_glu_t, l, 0, 2 * conv_w, F32)
        k_p = to_cache_layout(k, 0, mp, n_heads, l, depth, into=k_p)
        v_p = to_cache_layout(v, 0, mp, n_heads, l, depth, into=v_p)
        k_s = to_cache_layout(k, mp, ms, n_heads, l, depth, into=k_s)
        v_s = to_cache_layout(v, mp, ms, n_heads, l, depth, into=v_s)

        logf_p = logf[:mp].reshape(n_b, seq, n_heads)
        logf_s = logf[mp:].reshape(n_s, t_new, n_heads)
        cum_p = cumsum_time(logf_p)
        cum_s = cumsum_time(logf_s, offset=cum_cache[l][:, -1, :])

        att_p = flash_attention(q, kb, vb, cum_p.reshape(mp, n_heads), n_b, seq, n_heads)
        att = rmsnorm(att_p, row(g_att[l]), BF16, out_rows=mp + ms)
        cum_s2 = cum_s * LOG2E
        att = decode_attention(q, kb, vb, mp, ck, cv, l, cum_s2.reshape(ms, n_heads), cum_cache_s,
                               jnp.swapaxes(cum_s2, 1, 2), row(g_att[l]), att, n_s, t_new, n_heads)

        conv_args = (w_dw[l], row(b_dw[l]), row(cln_g[l]), row(cln_b[l]), row(g_conv[l]))
        conv, st_p = conv_branch(glu, 0, zero_prefix, *conv_args, n_b, seq, mp + ms)
        prefix_s = jnp.pad(state_conv[l], ((0, 0), (HALO - (CONV_WIDTH - 1), 0), (0, 0)))
        conv, st_s = conv_branch(glu, mp, prefix_s, *conv_args, n_s, t_new, mp + ms, into=conv)

        x = out_proj(att, conv, w_out, l, x)

        h2 = rmsnorm(x, row(ln_ffn[l]), BF16)
        hidden = ffn_up(h2, w_gate, w_up, l)
        x = ffn_down(hidden, w_down_b, l, x)

        outs["fp"].append(logf_p)
        outs["cp"].append(st_p)
        outs["fs"].append(logf_s)
        outs["cs"].append(st_s)

    y_p = rmsnorm(x, row(g_final), F32, 0, mp)
    y_s = rmsnorm(x, row(g_final), F32, mp, ms)
    stk = lambda k: jnp.stack(outs[k])
    prompt_kv = lambda a: a.reshape(depth, n_b, seq, n_heads, hd)
    sample_kv = lambda a: a.reshape(depth, n_s, t_new, n_heads, hd)
    return (y_p.reshape(n_b, seq, d), y_s.reshape(n_s, t_new, d),
            prompt_kv(k_p), prompt_kv(v_p), stk("fp"), stk("cp"),
            sample_kv(k_s), sample_kv(v_s), stk("fs"), stk("cs"))
```

```python
import functools
import math

import jax
import jax.numpy as jnp
from jax import lax
from jax.experimental import pallas as pl
from jax.experimental.pallas import tpu as pltpu

HEAD_DIM = 128
CONV_WIDTH = 31
HALO = 32
EPS = 1e-6
NEG = -1e30
LOG2E = math.log2(math.e)
LANES = 128
SUBLANES = 8
CONV_ROW_STRIDE = 4
CONV_GROUP = SUBLANES * CONV_ROW_STRIDE
VMEM_LIMIT = 60 * 1024 * 1024

BF16 = jnp.bfloat16
F32 = jnp.float32


def _tile(dim, pref, mult):
    t = min(pref, dim)
    t -= t % mult
    while t >= mult:
        if dim % t == 0:
            return t
        t -= mult
    return dim


def _params(*sem):
    return pltpu.CompilerParams(dimension_semantics=sem, vmem_limit_bytes=VMEM_LIMIT)


def _dot(a, b):
    return jnp.dot(a, b, preferred_element_type=F32)


def _dot_nt(a, b):
    return lax.dot_general(a, b, (((1,), (1,)), ((), ())), preferred_element_type=F32)


def _log_sigmoid(x):
    return jnp.minimum(x, 0.0) - jnp.log1p(jnp.exp(-jnp.abs(x)))


def _rms_scale(x):
    return lax.rsqrt(jnp.mean(x * x, axis=-1, keepdims=True) + EPS)


def _norm_gate_kernel(x_ref, g_ref, wf_ref, bf_ref, h_ref, lf_ref):
    x = x_ref[...]
    h = (x * _rms_scale(x) * g_ref[...]).astype(BF16)
    h_ref[...] = h
    lf_ref[...] = _log_sigmoid(_dot_nt(h, wf_ref[...].astype(BF16)) + bf_ref[...])


def norm_gate(x, g, wt, layer, row0, bf, n_heads):
    m, d = x.shape
    tm = _tile(m, 256, 8)
    assert row0 % n_heads == 0 and n_heads % SUBLANES == 0
    return pl.pallas_call(
        _norm_gate_kernel,
        grid=(m // tm,),
        in_specs=[pl.BlockSpec((tm, d), lambda i: (i, 0)),
                  pl.BlockSpec((1, d), lambda i: (0, 0)),
                  pl.BlockSpec((None, n_heads, d), lambda i: (layer, row0 // n_heads, 0)),
                  pl.BlockSpec((1, n_heads), lambda i: (0, 0))],
        out_specs=[pl.BlockSpec((tm, d), lambda i: (i, 0)),
                   pl.BlockSpec((tm, n_heads), lambda i: (i, 0))],
        out_shape=[jax.ShapeDtypeStruct((m, d), BF16),
                   jax.ShapeDtypeStruct((m, n_heads), F32)],
        compiler_params=_params("parallel"),
        name="norm_gate",
    )(x, g, wt, bf)


def _rmsnorm_kernel(x_ref, g_ref, o_ref):
    x = x_ref[...]
    o_ref[...] = (x * _rms_scale(x) * g_ref[...]).astype(o_ref.dtype)


def rmsnorm(x, g, out_dtype, row0=0, rows=None, out_rows=None):
    d = x.shape[1]
    rows = x.shape[0] if rows is None else rows
    out_rows = rows if out_rows is None else out_rows
    tm = _tile(math.gcd(rows, row0) if row0 else rows, 256, 8)
    r0 = row0 // tm
    return pl.pallas_call(
        _rmsnorm_kernel,
        grid=(rows // tm,),
        in_specs=[pl.BlockSpec((tm, d), lambda i: (r0 + i, 0)),
                  pl.BlockSpec((1, d), lambda i: (0, 0))],
        out_specs=pl.BlockSpec((tm, d), lambda i: (i, 0)),
        out_shape=jax.ShapeDtypeStruct((out_rows, d), out_dtype),
        compiler_params=_params("parallel"),
        name="rmsnorm",
    )(x, g)


def _cumsum_kernel(x_ref, off_ref, o_ref, *, rows_per_seq, n_heads):
    x = x_ref[...]
    tb = x.shape[0]
    head_bits = n_heads.bit_length() - 1
    seq_bits = rows_per_seq.bit_length() - 1
    li = lax.broadcasted_iota(jnp.int32, (LANES, LANES), 0)
    lj = lax.broadcasted_iota(jnp.int32, (LANES, LANES), 1)
    same_head = (li & (n_heads - 1)) == (lj & (n_heads - 1))
    upper = (same_head & ((li >> head_bits) <= (lj >> head_bits))).astype(F32)
    every = same_head.astype(F32)
    hi = lax.Precision.HIGHEST
    in_row = jnp.dot(x, upper, precision=hi, preferred_element_type=F32)
    row_tot = jnp.dot(x, every, precision=hi, preferred_element_type=F32)
    ri = lax.broadcasted_iota(jnp.int32, (tb, tb), 0)
    rj = lax.broadcasted_iota(jnp.int32, (tb, tb), 1)
    earlier = (((ri >> seq_bits) == (rj >> seq_bits)) & (rj < ri)).astype(F32)
    before = jnp.dot(earlier, row_tot, precision=hi, preferred_element_type=F32)
    o_ref[...] = in_row + before + off_ref[...]


def cumsum_time(logf, offset=None):
    b, t, h = logf.shape
    per_row = LANES // h
    rows = t // per_row
    assert h & (h - 1) == 0 and rows & (rows - 1) == 0 and rows * per_row == t
    x = logf.reshape(b * rows, LANES)
    if offset is None:
        off = jnp.zeros((b * rows, LANES), F32)
    else:
        off = jnp.broadcast_to(offset[:, None, None, :], (b, rows, per_row, h)).reshape(b * rows, LANES)
    tb = rows * _tile(b, max(1, 512 // rows), 1)
    out = pl.pallas_call(
        functools.partial(_cumsum_kernel, rows_per_seq=rows, n_heads=h),
        grid=(b * rows // tb,),
        in_specs=[pl.BlockSpec((tb, LANES), lambda i: (i, 0)),
                  pl.BlockSpec((tb, LANES), lambda i: (i, 0))],
        out_specs=pl.BlockSpec((tb, LANES), lambda i: (i, 0)),
        out_shape=jax.ShapeDtypeStruct((b * rows, LANES), F32),
        compiler_params=_params("parallel"),
        name="cumsum_time",
    )(x, off)
    return out.reshape(b, t, h)


def _cast_on_first_row_tile(w_refs, wb_refs):
    @pl.when(pl.program_id(1) == 0)
    def _():
        for w_ref, wb_ref in zip(w_refs, wb_refs):
            wb_ref[...] = w_ref[...].astype(BF16)


def _proj_kernel(x_ref, w_ref, o_ref, wb_s, *, out_scale):
    _cast_on_first_row_tile([w_ref], [wb_s])
    acc = _dot_nt(x_ref[...], wb_s[...])
    if out_scale != 1.0:
        acc = acc * out_scale
    o_ref[...] = acc.astype(o_ref.dtype)


def proj(x, wt, layer, row0, n, out_dtype, out_scale=1.0, tm=1536, tn=512):
    m, k = x.shape
    tm, tn = _tile(m, tm, 8), _tile(math.gcd(n, row0) if row0 else n, tn, LANES)
    r0 = row0 // tn
    return pl.pallas_call(
        functools.partial(_proj_kernel, out_scale=out_scale),
        grid=(n // tn, m // tm),
        in_specs=[pl.BlockSpec((tm, k), lambda j, i: (i, 0)),
                  pl.BlockSpec((None, tn, k), lambda j, i: (layer, r0 + j, 0))],
        out_specs=pl.BlockSpec((tm, tn), lambda j, i: (i, j)),
        out_shape=jax.ShapeDtypeStruct((m, n), out_dtype),
        scratch_shapes=[pltpu.VMEM((tn, k), BF16)],
        compiler_params=_params("arbitrary", "arbitrary"),
        name="proj",
    )(x, wt)


def _proj2_kernel(x_ref, w_ref, o_ref, ob_ref, wb_s):
    _cast_on_first_row_tile([w_ref], [wb_s])
    acc = _dot_nt(x_ref[...], wb_s[...])
    o_ref[...] = acc
    ob_ref[...] = acc.astype(BF16)


def proj2(x, wt, layer, row0, n, tm=1536, tn=512):
    m, k = x.shape
    tm, tn = _tile(m, tm, 8), _tile(math.gcd(n, row0) if row0 else n, tn, LANES)
    r0 = row0 // tn
    out = pl.BlockSpec((tm, tn), lambda j, i: (i, j))
    return pl.pallas_call(
        _proj2_kernel,
        grid=(n // tn, m // tm),
        in_specs=[pl.BlockSpec((tm, k), lambda j, i: (i, 0)),
                  pl.BlockSpec((None, tn, k), lambda j, i: (layer, r0 + j, 0))],
        out_specs=[out, out],
        out_shape=[jax.ShapeDtypeStruct((m, n), F32), jax.ShapeDtypeStruct((m, n), BF16)],
        scratch_shapes=[pltpu.VMEM((tn, k), BF16)],
        compiler_params=_params("arbitrary", "arbitrary"),
        name="proj2",
    )(x, wt)


def _cache_layout_kernel(*refs, n_heads):
    x_refs, o_ref = refs[:-1], refs[-1]
    tm = x_refs[0].shape[0]
    for d, x_ref in enumerate(x_refs):
        @pl.when(pl.program_id(0) == d)
        def _(x_ref=x_ref):
            for h in range(n_heads):
                o_ref[pl.ds(h, tm, stride=n_heads), :] = x_ref[:, h * HEAD_DIM:(h + 1) * HEAD_DIM]


def to_cache_layout(xs, row0, rows, n_heads, tm=256):
    tm = _tile(math.gcd(rows, row0) if row0 else rows, tm, 8)
    r0, nt = row0 // tm, rows // tm

    def x_spec(d):
        idle = lambda l: jnp.where(l > d, nt - 1, 0)
        return pl.BlockSpec((tm, n_heads * HEAD_DIM), lambda l, i: (r0 + jnp.where(l == d, i, idle(l)), 0))

    return pl.pallas_call(
        functools.partial(_cache_layout_kernel, n_heads=n_heads),
        grid=(len(xs), nt),
        in_specs=[x_spec(d) for d in range(len(xs))],
        out_specs=pl.BlockSpec((tm * n_heads, HEAD_DIM), lambda l, i: (l * nt + i, 0)),
        out_shape=jax.ShapeDtypeStruct((len(xs) * rows * n_heads, HEAD_DIM), xs[0].dtype),
        compiler_params=_params("arbitrary", "arbitrary"),
        name="to_cache_layout",
    )(*xs)


def _out_proj_kernel(ap_ref, cp_ref, as_ref, cs_ref, wa_ref, wc_ref, x_ref, o_ref, wab_s, wcb_s, *, n_first):
    _cast_on_first_row_tile([wa_ref, wc_ref], [wab_s, wcb_s])

    def emit(a_ref, c_ref):
        o_ref[...] = x_ref[...] + _dot(a_ref[...], wab_s[...]) + _dot(c_ref[...], wcb_s[...])

    @pl.when(pl.program_id(1) < n_first)
    def _():
        emit(ap_ref, cp_ref)

    @pl.when(pl.program_id(1) >= n_first)
    def _():
        emit(as_ref, cs_ref)


def out_proj(att_p, conv_p, att_s, conv_s, w_out, layer, x, tm=1024, tn=512):
    mp, ka = att_p.shape
    ms = att_s.shape[0]
    n = w_out.shape[2]
    assert conv_p.shape == att_p.shape and conv_s.shape == att_s.shape and x.shape[0] == mp + ms
    tm, tn = _tile(math.gcd(mp, ms), tm, 8), _tile(n, tn, LANES)
    n_first = mp // tm
    first = pl.BlockSpec((tm, ka), lambda j, i: (jnp.minimum(i, n_first - 1), 0))
    second = pl.BlockSpec((tm, ka), lambda j, i: (jnp.maximum(i - n_first, 0), 0),
                          pipeline_mode=pl.Buffered(1))
    return pl.pallas_call(
        functools.partial(_out_proj_kernel, n_first=n_first),
        grid=(n // tn, (mp + ms) // tm),
        in_specs=[first, first, second, second,
                  pl.BlockSpec((None, ka, tn), lambda j, i: (layer, 0, j)),
                  pl.BlockSpec((None, ka, tn), lambda j, i: (layer, 1, j)),
                  pl.BlockSpec((tm, tn), lambda j, i: (i, j))],
        out_specs=pl.BlockSpec((tm, tn), lambda j, i: (i, j)),
        out_shape=jax.ShapeDtypeStruct((mp + ms, n), F32),
        scratch_shapes=[pltpu.VMEM((ka, tn), BF16), pltpu.VMEM((ka, tn), BF16)],
        compiler_params=_params("arbitrary", "arbitrary"),
        name="out_proj",
    )(att_p, conv_p, att_s, conv_s, w_out, w_out, x)


def _ffn_up_kernel(h_ref, wg_ref, wu_ref, o_ref, wgb_s, wub_s):
    _cast_on_first_row_tile([wg_ref, wu_ref], [wgb_s, wub_s])
    h = h_ref[...]
    g = _dot(h, wgb_s[...])
    u = _dot(h, wub_s[...])
    o_ref[...] = (g * jax.nn.sigmoid(g) * u).astype(o_ref.dtype)


def ffn_up(h, w_gate, w_up, layer, tm=1536, tn=256):
    m, k = h.shape
    n = w_gate.shape[2]
    tm, tn = _tile(m, tm, 8), _tile(n, tn, LANES)
    wspec = pl.BlockSpec((None, k, tn), lambda j, i: (layer, 0, j))
    return pl.pallas_call(
        _ffn_up_kernel,
        grid=(n // tn, m // tm),
        in_specs=[pl.BlockSpec((tm, k), lambda j, i: (i, 0)), wspec, wspec],
        out_specs=pl.BlockSpec((tm, tn), lambda j, i: (i, j)),
        out_shape=jax.ShapeDtypeStruct((m, n), BF16),
        scratch_shapes=[pltpu.VMEM((k, tn), BF16), pltpu.VMEM((k, tn), BF16)],
        compiler_params=_params("arbitrary", "arbitrary"),
        name="ffn_up",
    )(h, w_gate, w_up)


def _ffn_down_kernel(h_ref, w_ref, x_ref, o_ref):
    o_ref[...] = x_ref[...] + _dot(h_ref[...], w_ref[...])


def ffn_down(hidden, wd, layer, x, tm=512, tn=512):
    m, k = hidden.shape
    n = wd.shape[2]
    tm, tn = _tile(m, tm, 8), _tile(n, tn, LANES)
    return pl.pallas_call(
        _ffn_down_kernel,
        grid=(m // tm, n // tn),
        in_specs=[pl.BlockSpec((tm, k), lambda i, j: (i, 0)),
                  pl.BlockSpec((None, k, tn), lambda i, j: (layer, 0, j)),
                  pl.BlockSpec((tm, tn), lambda i, j: (i, j))],
        out_specs=pl.BlockSpec((tm, tn), lambda i, j: (i, j)),
        out_shape=jax.ShapeDtypeStruct((m, n), F32),
        compiler_params=_params("parallel", "parallel"),
        name="ffn_down",
    )(hidden, wd, x)


def _causal(s):
    row = lax.broadcasted_iota(jnp.int32, s.shape, 0)
    col = lax.broadcasted_iota(jnp.int32, s.shape, 1)
    return jnp.where(col <= row, s, NEG)


def _column(x, h):
    lane = lax.broadcasted_iota(jnp.int32, x.shape, 1)
    return jnp.sum(jnp.where(lane == h, x, 0.0), axis=-1, keepdims=True)


def _split3(x):
    hi = x.astype(BF16).astype(F32)
    r = x - hi
    mid = r.astype(BF16).astype(F32)
    lo = (r - mid).astype(BF16).astype(F32)
    return hi, mid, lo


def _bias_lanes(d, query_side):
    hi, mid, lo = _split3(d)
    lane = lax.broadcasted_iota(jnp.int32, (d.shape[0], LANES), 1)
    one = jnp.where(lane < 6, 1.0, 0.0)
    if query_side:
        v = jnp.where(lane == 0, hi, jnp.where(lane == 1, mid, jnp.where(lane == 2, lo, one)))
    else:
        v = jnp.where(lane == 3, -hi, jnp.where(lane == 4, -mid, jnp.where(lane == 5, -lo, one)))
    return v.astype(BF16)


def _flash_kernel(q_ref, k_ref, v_ref, c_ref, o_ref, qa_s, qb_s, ka_s, va_s, m_s, acc_s, *, tq, halves):
    h = pl.program_id(1)
    qi = pl.program_id(2)
    hd = HEAD_DIM

    @pl.when(qi == 0)
    def _():
        d = _column(c_ref[...], h) * LOG2E
        ka_s[:, :hd] = k_ref[...]
        ka_s[:, hd:] = _bias_lanes(d, False)
        qb_s[...] = _bias_lanes(d, True)
        va_s[:, :hd] = v_ref[...]
        lane = lax.broadcasted_iota(jnp.int32, (va_s.shape[0], LANES), 1)
        va_s[:, hd:] = jnp.where(lane == 0, 1.0, 0.0).astype(BF16)

    qa_s[:, :hd] = q_ref[...]
    qa_s[:, hd:] = qb_s[pl.ds(pl.multiple_of(qi * tq, tq), tq), :]
    m_s[...] = jnp.full(m_s.shape, NEG, F32)
    acc_s[...] = jnp.zeros(acc_s.shape, F32)
    th = tq // halves

    def scores(ki):
        k0 = pl.multiple_of(ki * tq, tq)
        ka = ka_s[pl.ds(k0, tq), :]
        return [_dot_nt(qa_s[r * th:(r + 1) * th, :], ka) for r in range(halves)]

    def update(ki, s_halves, masked):
        k0 = pl.multiple_of(ki * tq, tq)
        va = va_s[pl.ds(k0, tq), :]
        for r, s in enumerate(s_halves):
            rows = slice(r * th, (r + 1) * th)
            if masked:
                row = lax.broadcasted_iota(jnp.int32, s.shape, 0) + r * th
                col = lax.broadcasted_iota(jnp.int32, s.shape, 1)
                s = jnp.where(col <= row, s, NEG)
            m_prev = m_s[rows, :]
            m_new = jnp.maximum(m_prev, jnp.max(s, axis=-1, keepdims=True))
            p = jnp.exp2(s - m_new).astype(BF16)
            acc_s[rows, :] = jnp.exp2(m_prev - m_new) * acc_s[rows, :] + _dot(p, va)
            m_s[rows, :] = m_new

    def body(ki, s_halves):
        s_next = scores(ki + 1)
        update(ki, s_halves, False)
        return s_next

    s_last = lax.fori_loop(0, qi, body, scores(0))
    update(qi, s_last, True)
    acc = acc_s[...]
    o_ref[...] = acc[:, :hd] / acc[:, hd:hd + 1]


def flash_attention(q, k, v, cum, n_batch, seq, n_heads, tq=1024):
    tq = _tile(seq, tq, LANES)
    nq = seq // tq
    hd = HEAD_DIM
    kernel = functools.partial(_flash_kernel, tq=tq, halves=4)
    return pl.pallas_call(
        kernel,
        grid=(n_batch, n_heads, nq),
        in_specs=[pl.BlockSpec((tq, hd), lambda b, h, qi: (b * nq + qi, h)),
                  pl.BlockSpec((seq, hd), lambda b, h, qi: (b, h)),
                  pl.BlockSpec((seq, hd), lambda b, h, qi: (b, h)),
                  pl.BlockSpec((seq, n_heads), lambda b, h, qi: (b, 0))],
        out_specs=pl.BlockSpec((tq, hd), lambda b, h, qi: (b * nq + qi, h)),
        out_shape=jax.ShapeDtypeStruct((n_batch * seq, n_heads * hd), F32),
        scratch_shapes=[pltpu.VMEM((tq, 2 * hd), BF16),
                        pltpu.VMEM((seq, hd), BF16),
                        pltpu.VMEM((seq, 2 * hd), BF16),
                        pltpu.VMEM((seq, 2 * hd), BF16),
                        pltpu.VMEM((tq, 1), F32),
                        pltpu.VMEM((tq, 2 * hd), F32)],
        compiler_params=_params("parallel", "parallel", "arbitrary"),
        name="flash_attention",
    )(q, k, v, cum)


def _decode_kernel(q_ref, kn_ref, vn_ref, kc_ref, vc_ref, cq_ref, ckc_ref, ckn_ref, g_ref, o_ref,
                   m_s, l_s, acc_s, *, n_heads, t_new, tk):
    kb = pl.program_id(1)
    hd = HEAD_DIM
    hps = n_heads // SUBLANES
    n_rows = tk * hps

    @pl.when(kb == 0)
    def _():
        m_s[...] = jnp.full(m_s.shape, NEG, F32)
        l_s[...] = jnp.zeros(l_s.shape, F32)
        acc_s[...] = jnp.zeros(acc_s.shape, F32)

    def online_update(rows, p, s, v):
        cols = slice(p * hd, (p + 1) * hd)
        m_prev = m_s[p, rows, :]
        m_new = jnp.maximum(m_prev, jnp.max(s, axis=-1, keepdims=True))
        alpha = jnp.exp2(m_prev - m_new)
        e = jnp.exp2(s - m_new)
        l_s[p, rows, :] = alpha * l_s[p, rows, :] + jnp.sum(e, axis=-1, keepdims=True)
        acc_s[rows, cols] = alpha * acc_s[rows, cols] + _dot(e.astype(BF16), v)
        m_s[p, rows, :] = m_new

    if hps > 1:
        shape = (hps * t_new, n_rows)
        row = lax.broadcasted_iota(jnp.int32, shape, 0)
        col = lax.broadcasted_iota(jnp.int32, shape, 1)
        row_head = sum((row >= r * t_new).astype(jnp.int32) for r in range(1, hps))
        own_head = row_head == (col & (hps - 1))

    heads_of = lambda p: [p + SUBLANES * r for r in range(hps)]
    for p in range(SUBLANES):
        q = jnp.concatenate([q_ref[:, h * hd:(h + 1) * hd] for h in heads_of(p)], axis=0)
        dq = jnp.concatenate([cq_ref[:, h:h + 1] for h in heads_of(p)], axis=0)
        k = kc_ref[pl.ds(p, n_rows, stride=SUBLANES), :].astype(BF16)
        v = vc_ref[pl.ds(p, n_rows, stride=SUBLANES), :].astype(BF16)
        s = _dot_nt(q, k) + (dq - ckc_ref[p:p + 1, :])
        if hps > 1:
            s = jnp.where(own_head, s, NEG)
        online_update(slice(0, hps * t_new), p, s, v)

    @pl.when(kb == pl.num_programs(1) - 1)
    def _():
        for h in range(n_heads):
            p, r = h % SUBLANES, h // SUBLANES
            cols = slice(h * hd, (h + 1) * hd)
            s = _dot_nt(q_ref[:, cols], kn_ref[:, cols]) + (cq_ref[:, h:h + 1] - ckn_ref[h:h + 1, :])
            online_update(slice(r * t_new, (r + 1) * t_new), p, _causal(s), vn_ref[:, cols])
        for p in range(SUBLANES):
            cols = slice(p * hd, (p + 1) * hd)
            acc_s[:, cols] = acc_s[:, cols] / l_s[p]
        att = jnp.concatenate([acc_s[r * t_new:(r + 1) * t_new, :] for r in range(hps)], axis=1)
        o_ref[...] = (att * _rms_scale(att) * g_ref[...]).astype(o_ref.dtype)


def decode_attention(q, k_new, v_new, row0, cache_k, cache_v, layer, cum_new, cum_cache_s, cum_new_t, g_att,
                     n_batch, t_new, n_heads, tk=1024):
    width = n_heads * HEAD_DIM
    past = cache_k.shape[2] // n_heads
    hps = n_heads // SUBLANES
    assert n_heads % SUBLANES == 0 and hps & (hps - 1) == 0
    tk = _tile(past, tk, LANES)
    r0 = row0 // t_new
    kernel = functools.partial(_decode_kernel, n_heads=n_heads, t_new=t_new, tk=tk)
    new_spec = pl.BlockSpec((t_new, width), lambda b, kb: (r0 + b, 0))
    cache_spec = pl.BlockSpec((None, None, tk * n_heads, HEAD_DIM), lambda b, kb: (layer, b, kb, 0))
    return pl.pallas_call(
        kernel,
        grid=(n_batch, past // tk),
        in_specs=[new_spec, new_spec, new_spec, cache_spec, cache_spec,
                  pl.BlockSpec((t_new, n_heads), lambda b, kb: (b, 0)),
                  pl.BlockSpec((None, None, SUBLANES, tk * hps), lambda b, kb: (layer, b, 0, kb)),
                  pl.BlockSpec((None, n_heads, t_new), lambda b, kb: (b, 0, 0)),
                  pl.BlockSpec((1, width), lambda b, kb: (0, 0))],
        out_specs=pl.BlockSpec((t_new, width), lambda b, kb: (b, 0)),
        out_shape=jax.ShapeDtypeStruct((n_batch * t_new, width), BF16),
        scratch_shapes=[pltpu.VMEM((SUBLANES, hps * t_new, 1), F32),
                        pltpu.VMEM((SUBLANES, hps * t_new, 1), F32),
                        pltpu.VMEM((hps * t_new, SUBLANES * HEAD_DIM), F32)],
        compiler_params=_params("parallel", "arbitrary"),
        name="decode_attention",
    )(q, k_new, v_new, cache_k, cache_v, cum_new, cum_cache_s, cum_new_t, g_att)


def _conv_kernel(a_ref, g_ref, pre_ref, w_ref, b_ref, lng_ref, lnb_ref, gc_ref, o_ref, st_ref, u_s, y_s, *, tt):
    t = pl.program_id(1)
    n_slabs = u_s.shape[0]
    slab = lambda c: slice(c * LANES, (c + 1) * LANES)

    @pl.when(t == 0)
    def _():
        for c in range(n_slabs):
            u_s[c, 0:HALO, :] = pre_ref[:, slab(c)]

    @pl.when(t > 0)
    def _():
        for c in range(n_slabs):
            u_s[c, 0:HALO, :] = u_s[c, tt:tt + HALO, :]

    u = a_ref[...] * jax.nn.sigmoid(g_ref[...])
    for c in range(n_slabs):
        u_s[c, HALO:HALO + tt, :] = u[:, slab(c)]

    first = HALO - (CONV_WIDTH - 1)

    def group(gi, carry):
        r0 = pl.multiple_of(gi * CONV_GROUP, CONV_GROUP)
        for c in range(n_slabs):
            taps = [w_ref[j:j + 1, slab(c)] for j in range(CONV_WIDTH)]
            win = [u_s[c, pl.ds(r0 + first + k, SUBLANES, stride=CONV_ROW_STRIDE), :]
                   for k in range(CONV_WIDTH + CONV_ROW_STRIDE - 1)]
            for ph in range(CONV_ROW_STRIDE):
                acc = jnp.broadcast_to(b_ref[:, slab(c)], (SUBLANES, LANES))
                for j in range(CONV_WIDTH):
                    acc = acc + win[j + ph] * taps[j]
                y_s[c, pl.ds(r0 + ph, SUBLANES, stride=CONV_ROW_STRIDE), :] = acc
        return carry

    lax.fori_loop(0, tt // CONV_GROUP, group, 0)

    y = jnp.concatenate([y_s[c] for c in range(n_slabs)], axis=1)
    mu = jnp.mean(y, axis=-1, keepdims=True)
    yc = y - mu
    y = yc * lax.rsqrt(jnp.mean(yc * yc, axis=-1, keepdims=True) + EPS) * lng_ref[...] + lnb_ref[...]
    y = y * jax.nn.sigmoid(y)
    o_ref[...] = (y * _rms_scale(y) * gc_ref[...]).astype(o_ref.dtype)

    @pl.when(t == pl.num_programs(1) - 1)
    def _():
        st_ref[...] = jnp.concatenate([u_s[c, tt + first:tt + HALO, :] for c in range(n_slabs)], axis=1)


def conv_branch(glu, row0, prefix, w_dw, b_dw, cln_g, cln_b, g_conv, n_seq, seq, tt=256):
    width = w_dw.shape[1]
    tt = _tile(math.gcd(seq, row0) if row0 else seq, tt, CONV_GROUP)
    assert tt % CONV_GROUP == 0 and width % LANES == 0
    nt = seq // tt
    r0 = row0 // tt
    kernel = functools.partial(_conv_kernel, tt=tt)
    vec = lambda n: pl.BlockSpec((n, width), lambda s, t: (0, 0))
    return pl.pallas_call(
        kernel,
        grid=(n_seq, nt),
        in_specs=[pl.BlockSpec((tt, width), lambda s, t: (r0 + s * nt + t, 0)),
                  pl.BlockSpec((tt, width), lambda s, t: (r0 + s * nt + t, 1)),
                  pl.BlockSpec((None, HALO, width), lambda s, t: (s, 0, 0)),
                  vec(CONV_WIDTH), vec(1), vec(1), vec(1), vec(1)],
        out_specs=[pl.BlockSpec((tt, width), lambda s, t: (s * nt + t, 0)),
                   pl.BlockSpec((None, CONV_WIDTH - 1, width), lambda s, t: (s, 0, 0))],
        out_shape=[jax.ShapeDtypeStruct((n_seq * seq, width), BF16),
                   jax.ShapeDtypeStruct((n_seq, CONV_WIDTH - 1, width), F32)],
        scratch_shapes=[pltpu.VMEM((width // LANES, tt + HALO, LANES), F32),
                        pltpu.VMEM((width // LANES, tt, LANES), F32)],
        compiler_params=_params("parallel", "arbitrary"),
        name="conv_branch",
    )(glu, glu, prefix, w_dw, b_dw, cln_g, cln_b, g_conv)


def kernel(x_prompt, x_sample, cache_k, cache_v, cache_logf, state_conv, ln_mix, w_in, b_f,
           w_dw, b_dw, cln_g, cln_b, g_att, g_conv, w_out, ln_ffn, w_gate, w_up, w_down, g_final):
    n_b, seq, d = x_prompt.shape
    n_s, t_new, _ = x_sample.shape
    depth, _, past, n_heads, hd = cache_k.shape
    assert hd == HEAD_DIM
    att_w = n_heads * hd
    conv_w = w_dw.shape[2]
    assert conv_w == att_w and w_in.shape[2] == 3 * att_w + n_heads + 2 * conv_w
    mp = n_b * seq
    ms = n_s * t_new
    row = lambda v: v.reshape(1, -1)
    q_scale = hd ** -0.5 * LOG2E

    x = jnp.concatenate([x_prompt.reshape(mp, d), x_sample.reshape(ms, d)], axis=0)
    cum_cache = cumsum_time(cache_logf.reshape(depth * n_s, past, n_heads)).reshape(depth, n_s, past, n_heads)
    hps = n_heads // SUBLANES
    cum_cache_s = jnp.swapaxes(cum_cache.reshape(depth, n_s, past * hps, SUBLANES), 2, 3) * LOG2E
    ck = cache_k.reshape(depth, n_s, past * n_heads, hd)
    cv = cache_v.reshape(depth, n_s, past * n_heads, hd)
    zero_prefix = jnp.zeros((n_b, HALO, conv_w), F32)
    gate0 = 3 * att_w
    w_in_t = jnp.swapaxes(w_in, 1, 2)
    w_glu_t = w_in_t[:, gate0 + n_heads:, :]
    w_down_b = w_down.astype(BF16)

    outs = {k: [] for k in ("fp", "cp", "fs", "cs")}
    ks, vs = [], []
    for l in range(depth):
        h, logf = norm_gate(x, row(ln_mix[l]), w_in_t, l, gate0, row(b_f[l]), n_heads)
        q = proj(h, w_in_t, l, 0, att_w, BF16, out_scale=q_scale)
        k, kb = proj2(h, w_in_t, l, att_w, att_w)
        v, vb = proj2(h, w_in_t, l, 2 * att_w, att_w)
        glu = proj(h, w_glu_t, l, 0, 2 * conv_w, F32)
        ks.append(k)
        vs.append(v)

        logf_p = logf[:mp].reshape(n_b, seq, n_heads)
        logf_s = logf[mp:].reshape(n_s, t_new, n_heads)
        cum_p = cumsum_time(logf_p)
        cum_s = cumsum_time(logf_s, offset=cum_cache[l][:, -1, :])

        att_p = flash_attention(q, kb, vb, cum_p.reshape(mp, n_heads), n_b, seq, n_heads)
        att_p = rmsnorm(att_p, row(g_att[l]), BF16)
        cum_s2 = cum_s * LOG2E
        att_s = decode_attention(q, kb, vb, mp, ck, cv, l, cum_s2.reshape(ms, n_heads), cum_cache_s,
                                 jnp.swapaxes(cum_s2, 1, 2), row(g_att[l]), n_s, t_new, n_heads)

        conv_args = (w_dw[l], row(b_dw[l]), row(cln_g[l]), row(cln_b[l]), row(g_conv[l]))
        conv_p, st_p = conv_branch(glu, 0, zero_prefix, *conv_args, n_b, seq)
        prefix_s = jnp.pad(state_conv[l], ((0, 0), (HALO - (CONV_WIDTH - 1), 0), (0, 0)))
        conv_s, st_s = conv_branch(glu, mp, prefix_s, *conv_args, n_s, t_new)

        x = out_proj(att_p, conv_p, att_s, conv_s, w_out, l, x)

        h2 = rmsnorm(x, row(ln_ffn[l]), BF16)
        hidden = ffn_up(h2, w_gate, w_up, l)
        x = ffn_down(hidden, w_down_b, l, x)

        outs["fp"].append(logf_p)
        outs["cp"].append(st_p)
        outs["fs"].append(logf_s)
        outs["cs"].append(st_s)

    y_p = rmsnorm(x, row(g_final), F32, 0, mp)
    y_s = rmsnorm(x, row(g_final), F32, mp, ms)
    stk = lambda k: jnp.stack(outs[k])
    prompt_kv = lambda a: to_cache_layout(a, 0, mp, n_heads).reshape(depth, n_b, seq, n_heads, hd)
    sample_kv = lambda a: to_cache_layout(a, mp, ms, n_heads).reshape(depth, n_s, t_new, n_heads, hd)
    return (y_p.reshape(n_b, seq, d), y_s.reshape(n_s, t_new, d),
            prompt_kv(ks), prompt_kv(vs), stk("fp"), stk("cp"),
            sample_kv(ks), sample_kv(vs), stk("fs"), stk("cs"))
```
